```python
import math
import jax, jax.numpy as jnp
from jax import lax
import numpy as np

D_MODEL = 4096
BATCH = 4
SEQ = 2048
DEPTH = 1

MIX_WIDTH = D_MODEL
ATTN_WIDTH = MIX_WIDTH // 2
CONV_WIDTH = MIX_WIDTH - ATTN_WIDTH
ATTN_HEAD_DIM = 64
ATTN_V_DIM = 2 * ATTN_HEAD_DIM
N_ATTN_HEADS = ATTN_WIDTH // ATTN_V_DIM
QK_WIDTH = N_ATTN_HEADS * 2 * ATTN_HEAD_DIM
CONV_K = 3
CONV_GROUPS = 16
IN_PROJ_WIDTH = 2 * QK_WIDTH + ATTN_WIDTH + 3 * CONV_WIDTH
D_FF = ((8 * D_MODEL // 3 + 255) // 256) * 256
ROPE_THETA = 10000.0
Q_BLOCK = 128
NORM_EPS = 1e-6
N_MOD = 6

kernel_name = "hybrid_diffattn_shortconv_adaln_block"


def lambda_init_fn(layer_idx):
    return 0.8 - 0.6 * math.exp(-0.3 * layer_idx)


def rms_norm(x, gain):
    xf = x.astype(jnp.float32)
    y = xf * lax.rsqrt(jnp.mean(xf * xf, axis=-1, keepdims=True) + NORM_EPS)
    return (y * gain.astype(jnp.float32)).astype(x.dtype)


def apply_rope(t, pos):
    half = t.shape[-1] // 2
    inv_freq = ROPE_THETA ** (-jnp.arange(half, dtype=jnp.float32) / half)
    ang = pos.astype(jnp.float32)[:, None] * inv_freq[None, :]
    cos = jnp.cos(ang)[None, :, None, :]
    sin = jnp.sin(ang)[None, :, None, :]
    tf = t.astype(jnp.float32)
    t1, t2 = tf[..., :half], tf[..., half:]
    out = jnp.concatenate([t1 * cos - t2 * sin, t2 * cos + t1 * sin], axis=-1)
    return out.astype(t.dtype)


def diff_attention(q1, q2, k1, k2, v, lam):
    b, h, s, dh = q1.shape
    dv = v.shape[-1]
    n_blk = s // Q_BLOCK
    scale = dh ** -0.5
    kpos = jnp.arange(s)
    neg = jnp.finfo(jnp.float32).min

    def block(args):
        qb1, qb2, start = args
        qpos = start + jnp.arange(Q_BLOCK)
        mask = kpos[None, :] <= qpos[:, None]

        def probs(qb, k):
            sc = jnp.einsum('bhqd,bhkd->bhqk', qb, k).astype(jnp.float32) * scale
            return jax.nn.softmax(jnp.where(mask, sc, neg), axis=-1)

        p = probs(qb1, k1) - lam * probs(qb2, k2)
        return jnp.einsum('bhqk,bhkd->bhqd', p.astype(v.dtype), v)

    def to_blocks(t):
        return t.reshape(b, h, n_blk, Q_BLOCK, dh).transpose(2, 0, 1, 3, 4)

    out = lax.map(block, (to_blocks(q1), to_blocks(q2), jnp.arange(n_blk) * Q_BLOCK))
    return out.transpose(1, 2, 0, 3, 4).reshape(b, h, s, dv)


def causal_depthwise_conv(u, w):
    return lax.conv_general_dilated(
        u, w[:, None, :].astype(u.dtype), window_strides=(1,),
        padding=[(CONV_K - 1, 0)], dimension_numbers=('NWC', 'WIO', 'NWC'),
        feature_group_count=u.shape[-1])


def setup_inputs(seed: int = 0) -> dict:
    key = jax.random.key(seed)
    ks = jax.random.split(key, 20)
    f32 = jnp.float32

    def nrm(k, shape, s):
        return jax.random.normal(k, shape, f32) * s

    return {
        "x": nrm(ks[0], (BATCH, SEQ, D_MODEL), 1.0),
        "c": nrm(ks[1], (BATCH, D_MODEL), 1.0),
        "w_ada": nrm(ks[2], (DEPTH, D_MODEL, N_MOD * D_MODEL), 0.5 * D_MODEL ** -0.5),
        "b_ada": nrm(ks[3], (DEPTH, N_MOD * D_MODEL), 0.02),
        "norm1_g": 1.0 + nrm(ks[4], (DEPTH, D_MODEL), 0.02),
        "w_in": nrm(ks[5], (DEPTH, D_MODEL, IN_PROJ_WIDTH), D_MODEL ** -0.5),
        "lambda_q1": nrm(ks[6], (DEPTH, ATTN_HEAD_DIM), 0.1),
        "lambda_k1": nrm(ks[7], (DEPTH, ATTN_HEAD_DIM), 0.1),
        "lambda_q2": nrm(ks[8], (DEPTH, ATTN_HEAD_DIM), 0.1),
        "lambda_k2": nrm(ks[9], (DEPTH, ATTN_HEAD_DIM), 0.1),
        "subln_g": 1.0 + nrm(ks[10], (DEPTH, ATTN_V_DIM), 0.02),
        "conv_w": nrm(ks[11], (DEPTH, CONV_K, CONV_WIDTH), CONV_K ** -0.5),
        "w_out": nrm(ks[12], (DEPTH, MIX_WIDTH, D_MODEL), MIX_WIDTH ** -0.5),
        "norm2_g": 1.0 + nrm(ks[13], (DEPTH, D_MODEL), 0.02),
        "w_gate": nrm(ks[14], (DEPTH, D_MODEL, D_FF), D_MODEL ** -0.5),
        "w_up": nrm(ks[15], (DEPTH, D_MODEL, D_FF), D_MODEL ** -0.5),
        "w_down": nrm(ks[16], (DEPTH, D_FF, D_MODEL), D_FF ** -0.5),
        "final_g": 1.0 + nrm(ks[17], (D_MODEL,), 0.02),
    }


def reference(x, c, w_ada, b_ada, norm1_g, w_in, lambda_q1, lambda_k1, lambda_q2,
              lambda_k2, subln_g, conv_w, w_out, norm2_g, w_gate, w_up, w_down, final_g):
    b, s, _ = x.shape
    pos = jnp.arange(s)
    c_act = jax.nn.silu(c)
    splits = np.cumsum([QK_WIDTH, QK_WIDTH, ATTN_WIDTH, CONV_WIDTH, CONV_WIDTH]).tolist()

    for l in range(DEPTH):
        lam_init = lambda_init_fn(l)
        mod = (c_act @ w_ada[l] + b_ada[l])[:, None, :]
        sh1, sc1, g1, sh2, sc2, g2 = jnp.split(mod, N_MOD, axis=-1)

        h = rms_norm(x, norm1_g[l]) * (1.0 + sc1) + sh1
        proj = h @ w_in[l]
        q, k, v, bg, cg, xg = jnp.split(proj, splits, axis=-1)

        q = q.reshape(b, s, N_ATTN_HEADS, 2, ATTN_HEAD_DIM)
        k = k.reshape(b, s, N_ATTN_HEADS, 2, ATTN_HEAD_DIM)
        v = v.reshape(b, s, N_ATTN_HEADS, ATTN_V_DIM)
        q1 = apply_rope(q[..., 0, :], pos).transpose(0, 2, 1, 3)
        q2 = apply_rope(q[..., 1, :], pos).transpose(0, 2, 1, 3)
        k1 = apply_rope(k[..., 0, :], pos).transpose(0, 2, 1, 3)
        k2 = apply_rope(k[..., 1, :], pos).transpose(0, 2, 1, 3)
        lam = (jnp.exp(jnp.sum(lambda_q1[l].astype(jnp.float32) * lambda_k1[l].astype(jnp.float32)))
               - jnp.exp(jnp.sum(lambda_q2[l].astype(jnp.float32) * lambda_k2[l].astype(jnp.float32)))
               + lam_init)
        attn = diff_attention(q1, q2, k1, k2, v.transpose(0, 2, 1, 3), lam)
        attn = rms_norm(attn, subln_g[l]) * (1.0 - lam_init)
        attn = attn.transpose(0, 2, 1, 3).reshape(b, s, ATTN_WIDTH)

        conv = bg * causal_depthwise_conv(cg * xg, conv_w[l])

        mix = jnp.concatenate([attn, conv], axis=-1) @ w_out[l]
        x = x + g1 * mix

        h = rms_norm(x, norm2_g[l]) * (1.0 + sc2) + sh2
        ffn = (jax.nn.silu(h @ w_gate[l]) * (h @ w_up[l])) @ w_down[l]
        x = x + g2 * ffn

    return rms_norm(x, final_g)
```

```python
import functools
import math

import jax
import jax.numpy as jnp
from jax import lax
from jax.experimental import pallas as pl
from jax.experimental.pallas import tpu as pltpu

F32 = jnp.float32
BF16 = jnp.bfloat16

ATTN_HEAD_DIM = 64
ATTN_V_DIM = 2 * ATTN_HEAD_DIM
CONV_K = 3
ROPE_THETA = 10000.0
NORM_EPS = 1e-6
N_MOD = 6
LAMBDA_INIT = 0.8 - 0.6 * math.exp(-0.3 * 0)

VMEM_LIMIT_BYTES = 60 * 1024 * 1024


def _params(*semantics):
    return pltpu.CompilerParams(dimension_semantics=semantics,
                                vmem_limit_bytes=VMEM_LIMIT_BYTES)


def _silu(v):
    return v * jax.nn.sigmoid(v)


def _ada_kernel(c_ref, w_ref, b_ref, o_ref):
    c_act = _silu(c_ref[...]).astype(BF16)
    o_ref[...] = jnp.dot(c_act, w_ref[...].astype(BF16),
                         preferred_element_type=F32) + b_ref[...]


def _ada(c_pad, w_ada, b_ada, tn=512):
    rows, d = c_pad.shape
    n = w_ada.shape[1]
    return pl.pallas_call(
        _ada_kernel,
        grid=(n // tn,),
        in_specs=[pl.BlockSpec((rows, d), lambda j: (0, 0)),
                  pl.BlockSpec((d, tn), lambda j: (0, j)),
                  pl.BlockSpec((1, tn), lambda j: (0, j))],
        out_specs=pl.BlockSpec((rows, tn), lambda j: (0, j)),
        out_shape=jax.ShapeDtypeStruct((rows, n), F32),
        compiler_params=_params("arbitrary"),
        name="ada_mod",
    )(c_pad, w_ada, b_ada)


def _norm_mod_kernel(x_ref, g_ref, sc_ref, sh_ref, o_ref):
    x = x_ref[...]
    y = x * lax.rsqrt(jnp.mean(x * x, axis=-1, keepdims=True) + NORM_EPS) * g_ref[...]
    o_ref[...] = (y * (1.0 + sc_ref[...]) + sh_ref[...]).astype(o_ref.dtype)


def _norm_mod(x2d, gain, mod4, sc_idx, sh_idx, seq, tm=512):
    m, d = x2d.shape
    per_batch = seq // tm
    return pl.pallas_call(
        _norm_mod_kernel,
        grid=(m // tm,),
        in_specs=[pl.BlockSpec((tm, d), lambda i: (i, 0)),
                  pl.BlockSpec((1, d), lambda i: (0, 0)),
                  pl.BlockSpec((None, None, 1, d), lambda i: (i // per_batch, sc_idx, 0, 0)),
                  pl.BlockSpec((None, None, 1, d), lambda i: (i // per_batch, sh_idx, 0, 0))],
        out_specs=pl.BlockSpec((tm, d), lambda i: (i, 0)),
        out_shape=jax.ShapeDtypeStruct((m, d), BF16),
        compiler_params=_params("arbitrary"),
        name="norm_mod",
    )(x2d, gain, mod4, mod4)


def _matmul_kernel(a_ref, w_ref, o_ref):
    o_ref[...] = jnp.dot(a_ref[...], w_ref[...].astype(BF16),
                         preferred_element_type=F32).astype(o_ref.dtype)


def _matmul(a, w, tm=1024, tn=512):
    m, k = a.shape
    n = w.shape[1]
    return pl.pallas_call(
        _matmul_kernel,
        grid=(m // tm, n // tn),
        in_specs=[pl.BlockSpec((tm, k), lambda i, j: (i, 0)),
                  pl.BlockSpec((k, tn), lambda i, j: (0, j))],
        out_specs=pl.BlockSpec((tm, tn), lambda i, j: (i, j)),
        out_shape=jax.ShapeDtypeStruct((m, n), BF16),
        compiler_params=_params("arbitrary", "arbitrary"),
        name="in_proj",
    )(a, w)


def _attn_kernel(q_ref, k_ref, v_ref, cos_ref, sa_ref, sb_ref, lq1_ref, lk1_ref,
                 lq2_ref, lk2_ref, g_ref, o_ref, q1_s, q2_s, k_s, *, tq):
    seq, width = q_ref.shape
    half = ATTN_HEAD_DIM // 2
    cos, sa, sb = cos_ref[...], sa_ref[...], sb_ref[...]

    def rope(t):
        return (t * cos + pltpu.roll(t, width - half, 1) * sa
                + pltpu.roll(t, half, 1) * sb)

    lane = lax.broadcasted_iota(jnp.int32, (seq, width), 1)
    first = lane < ATTN_HEAD_DIM
    q = rope(q_ref[...].astype(F32)) * (ATTN_HEAD_DIM ** -0.5)
    q1_s[...] = jnp.where(first, q, 0.0).astype(BF16)
    q2_s[...] = jnp.where(first, 0.0, q).astype(BF16)
    k_s[...] = rope(k_ref[...].astype(F32)).astype(BF16)

    lam = (jnp.exp(jnp.sum(lq1_ref[...] * lk1_ref[...], keepdims=True))
           - jnp.exp(jnp.sum(lq2_ref[...] * lk2_ref[...], keepdims=True))
           + LAMBDA_INIT)
    neg = jnp.finfo(F32).min
    gain = g_ref[...] * (1.0 - LAMBDA_INIT)

    for qi in range(seq // tq):
        r0 = qi * tq
        kv_len = r0 + tq
        kb = k_s[0:kv_len, :]
        row = r0 + lax.broadcasted_iota(jnp.int32, (tq, kv_len), 0)
        col = lax.broadcasted_iota(jnp.int32, (tq, kv_len), 1)
        mask = col <= row

        def probs(qz):
            s = lax.dot_general(qz, kb, (((1,), (1,)), ((), ())),
                                preferred_element_type=F32)
            s = jnp.where(mask, s, neg)
            e = jnp.exp(s - jnp.max(s, axis=-1, keepdims=True))
            return e * (1.0 / jnp.sum(e, axis=-1, keepdims=True))

        p = probs(q1_s[r0:r0 + tq, :]) - lam * probs(q2_s[r0:r0 + tq, :])
        o = jnp.dot(p.astype(BF16), v_ref[0:kv_len, :], preferred_element_type=F32)
        o = o * lax.rsqrt(jnp.mean(o * o, axis=-1, keepdims=True) + NORM_EPS) * gain
        o_ref[r0:r0 + tq, :] = o.astype(o_ref.dtype)


def _attention(proj, tables, lams, subln_g, batch, seq, n_heads, mix_width, tq=256):
    dv = ATTN_V_DIM
    cos, sa, sb = tables
    full = lambda b, h: (0, 0)
    vec = pl.BlockSpec((1, ATTN_HEAD_DIM), full)
    return pl.pallas_call(
        functools.partial(_attn_kernel, tq=tq),
        grid=(batch, n_heads),
        in_specs=[pl.BlockSpec((seq, dv), lambda b, h: (b, h)),
                  pl.BlockSpec((seq, dv), lambda b, h: (b, n_heads + h)),
                  pl.BlockSpec((seq, dv), lambda b, h: (b, 2 * n_heads + h)),
                  pl.BlockSpec((seq, dv), full), pl.BlockSpec((seq, dv), full),
                  pl.BlockSpec((seq, dv), full),
                  vec, vec, vec, vec,
                  pl.BlockSpec((1, dv), full)],
        out_specs=pl.BlockSpec((seq, dv), lambda b, h: (b, h)),
        out_shape=jax.ShapeDtypeStruct((batch * seq, mix_width), BF16),
        scratch_shapes=[pltpu.VMEM((seq, dv), BF16)] * 3,
        compiler_params=_params("arbitrary", "arbitrary"),
        name="diff_attn",
    )(proj, proj, proj, cos, sa, sb, *lams, subln_g)


def _conv_kernel(mix_ref, bg_ref, cg_ref, xg_ref, w_ref, o_ref):
    del mix_ref
    u = cg_ref[...].astype(F32) * xg_ref[...].astype(F32)
    w = w_ref[...]
    row = lax.broadcasted_iota(jnp.int32, u.shape, 0)
    y = w[CONV_K - 1:CONV_K, :] * u
    for shift in range(1, CONV_K):
        shifted = jnp.where(row >= shift, pltpu.roll(u, shift, 0), 0.0)
        y = y + w[CONV_K - 1 - shift:CONV_K - shift, :] * shifted
    o_ref[...] = (bg_ref[...].astype(F32) * y).astype(o_ref.dtype)


def _conv(mix, proj, conv_w, batch, seq, attn_width, conv_width, tc=512):
    qkv_blocks = 3 * attn_width // tc
    cblocks = conv_width // tc
    return pl.pallas_call(
        _conv_kernel,
        grid=(batch, cblocks),
        in_specs=[pl.BlockSpec(memory_space=pl.ANY),
                  pl.BlockSpec((seq, tc), lambda b, j: (b, qkv_blocks + j)),
                  pl.BlockSpec((seq, tc), lambda b, j: (b, qkv_blocks + cblocks + j)),
                  pl.BlockSpec((seq, tc), lambda b, j: (b, qkv_blocks + 2 * cblocks + j)),
                  pl.BlockSpec((CONV_K, tc), lambda b, j: (0, j))],
        out_specs=pl.BlockSpec((seq, tc), lambda b, j: (b, attn_width // tc + j)),
        out_shape=jax.ShapeDtypeStruct(mix.shape, mix.dtype),
        input_output_aliases={0: 0},
        compiler_params=_params("arbitrary", "arbitrary"),
        name="gated_conv",
    )(mix, proj, proj, proj, conv_w)


def _out_proj_kernel(a_ref, w_ref, x_ref, g_ref, o_ref):
    acc = jnp.dot(a_ref[...], w_ref[...].astype(BF16), preferred_element_type=F32)
    o_ref[...] = x_ref[...] + g_ref[...] * acc


def _out_proj(mix, w_out, x2d, mod4, gate_idx, seq, tm=1024, tn=512):
    m, k = mix.shape
    n = w_out.shape[1]
    per_batch = seq // tm
    return pl.pallas_call(
        _out_proj_kernel,
        grid=(m // tm, n // tn),
        in_specs=[pl.BlockSpec((tm, k), lambda i, j: (i, 0)),
                  pl.BlockSpec((k, tn), lambda i, j: (0, j)),
                  pl.BlockSpec((tm, tn), lambda i, j: (i, j)),
                  pl.BlockSpec((None, None, 1, tn),
                               lambda i, j: (i // per_batch, gate_idx, 0, j))],
        out_specs=pl.BlockSpec((tm, tn), lambda i, j: (i, j)),
        out_shape=jax.ShapeDtypeStruct((m, n), F32),
        compiler_params=_params("arbitrary", "arbitrary"),
        name="out_proj",
    )(mix, w_out, x2d, mod4)


def _ffn_kernel(h_ref, wg_ref, wu_ref, wd_ref, o_ref, acc_ref):
    f = pl.program_id(1)

    @pl.when(f == 0)
    def _():
        acc_ref[...] = jnp.zeros_like(acc_ref)

    h = h_ref[...]
    g = jnp.dot(h, wg_ref[...].astype(BF16), preferred_element_type=F32)
    u = jnp.dot(h, wu_ref[...].astype(BF16), preferred_element_type=F32)
    a = (_silu(g) * u).astype(BF16)
    acc_ref[...] += jnp.dot(a, wd_ref[...].astype(BF16), preferred_element_type=F32)

    @pl.when(f == pl.num_programs(1) - 1)
    def _():
        o_ref[...] = acc_ref[...].astype(o_ref.dtype)


def _ffn(h, w_gate, w_up, w_down, tm=512, tf=256):
    m, d = h.shape
    d_ff = w_gate.shape[1]
    return pl.pallas_call(
        _ffn_kernel,
        grid=(m // tm, d_ff // tf),
        in_specs=[pl.BlockSpec((tm, d), lambda i, f: (i, 0)),
                  pl.BlockSpec((d, tf), lambda i, f: (0, f)),
                  pl.BlockSpec((d, tf), lambda i, f: (0, f)),
                  pl.BlockSpec((tf, d), lambda i, f: (f, 0))],
        out_specs=pl.BlockSpec((tm, d), lambda i, f: (i, 0)),
        out_shape=jax.ShapeDtypeStruct((m, d), BF16),
        scratch_shapes=[pltpu.VMEM((tm, d), F32)],
        compiler_params=_params("arbitrary", "arbitrary"),
        name="swiglu",
    )(h, w_gate, w_up, w_down)


def _final_kernel(x_ref, f_ref, g2_ref, gain_ref, o_ref):
    x = x_ref[...] + g2_ref[...] * f_ref[...].astype(F32)
    o_ref[...] = (x * lax.rsqrt(jnp.mean(x * x, axis=-1, keepdims=True) + NORM_EPS)
                  * gain_ref[...])


def _final(x1, ffn, mod4, gate_idx, final_g, seq, tm=512):
    m, d = x1.shape
    per_batch = seq // tm
    return pl.pallas_call(
        _final_kernel,
        grid=(m // tm,),
        in_specs=[pl.BlockSpec((tm, d), lambda i: (i, 0)),
                  pl.BlockSpec((tm, d), lambda i: (i, 0)),
                  pl.BlockSpec((None, None, 1, d), lambda i: (i // per_batch, gate_idx, 0, 0)),
                  pl.BlockSpec((1, d), lambda i: (0, 0))],
        out_specs=pl.BlockSpec((tm, d), lambda i: (i, 0)),
        out_shape=jax.ShapeDtypeStruct((m, d), F32),
        compiler_params=_params("arbitrary"),
        name="final_norm",
    )(x1, ffn, mod4, final_g)


def _rope_tables(seq):
    half = ATTN_HEAD_DIM // 2
    inv_freq = ROPE_THETA ** (-jnp.arange(half, dtype=F32) / half)
    ang = jnp.arange(seq, dtype=F32)[:, None] * inv_freq[None, :]
    cos, sin, zero = jnp.cos(ang), jnp.sin(ang), jnp.zeros_like(ang)
    reps = ATTN_V_DIM // ATTN_HEAD_DIM
    cos_t = jnp.tile(cos, (1, 2 * reps))
    sin_lo = jnp.tile(jnp.concatenate([-sin, zero], axis=1), (1, reps))
    sin_hi = jnp.tile(jnp.concatenate([zero, sin], axis=1), (1, reps))
    return cos_t, sin_lo, sin_hi


def kernel(x, c, w_ada, b_ada, norm1_g, w_in, lambda_q1, lambda_k1, lambda_q2, lambda_k2,
           subln_g, conv_w, w_out, norm2_g, w_gate, w_up, w_down, final_g):
    batch, seq, d = x.shape
    depth = w_ada.shape[0]
    assert depth == 1
    conv_width = conv_w.shape[-1]
    mix_width = w_out.shape[1]
    attn_width = mix_width - conv_width
    n_heads = attn_width // ATTN_V_DIM
    x2d = x.reshape(batch * seq, d)
    tables = _rope_tables(seq)

    for l in range(depth):
        c_pad = jnp.pad(c, ((0, 8 - batch), (0, 0)))
        mod = _ada(c_pad, w_ada[l], b_ada[l][None, :])
        mod4 = mod[:batch].reshape(batch, N_MOD, 1, d)

        h = _norm_mod(x2d, norm1_g[l][None, :], mod4, 1, 0, seq)
        proj = _matmul(h, w_in[l])
        lams = [v[l][None, :] for v in (lambda_q1, lambda_k1, lambda_q2, lambda_k2)]
        mix = _attention(proj, tables, lams, subln_g[l][None, :], batch, seq, n_heads,
                         mix_width)
        mix = _conv(mix, proj, conv_w[l], batch, seq, attn_width, conv_width)
        x2d = _out_proj(mix, w_out[l], x2d, mod4, 2, seq)

        h = _norm_mod(x2d, norm2_g[l][None, :], mod4, 4, 3, seq)
        ffn = _ffn(h, w_gate[l], w_up[l], w_down[l])
        x2d = _final(x2d, ffn, mod4, 5, final_g[None, :], seq)

    return x2d.reshape(batch, seq, d)
```

```python
import functools
import math

import jax
import jax.numpy as jnp
from jax import lax
from jax.experimental import pallas as pl
from jax.experimental.pallas import tpu as pltpu

F32 = jnp.float32
BF16 = jnp.bfloat16

ATTN_HEAD_DIM = 64
ATTN_V_DIM = 2 * ATTN_HEAD_DIM
CONV_K = 3
ROPE_THETA = 10000.0
NORM_EPS = 1e-6
N_MOD = 6
LAMBDA_INIT = 0.8 - 0.6 * math.exp(-0.3 * 0)

VMEM_LIMIT_BYTES = 60 * 1024 * 1024
LANES = 128
BF16_SUBLANES = 16


def _params(*semantics):
    return pltpu.CompilerParams(dimension_semantics=semantics,
                                vmem_limit_bytes=VMEM_LIMIT_BYTES)


def _silu(v):
    return v * jax.nn.sigmoid(v)


def _ada_kernel(c_ref, w_ref, b_ref, o_ref):
    c_act = _silu(c_ref[...]).astype(BF16)
    o_ref[...] = jnp.dot(c_act, w_ref[...].astype(BF16),
                         preferred_element_type=F32) + b_ref[...]


def _ada(c_pad, w_ada, b_ada, tn=512):
    rows, d = c_pad.shape
    n = w_ada.shape[1]
    return pl.pallas_call(
        _ada_kernel,
        grid=(n // tn,),
        in_specs=[pl.BlockSpec((rows, d), lambda j: (0, 0)),
                  pl.BlockSpec((d, tn), lambda j: (0, j)),
                  pl.BlockSpec((1, tn), lambda j: (0, j))],
        out_specs=pl.BlockSpec((rows, tn), lambda j: (0, j)),
        out_shape=jax.ShapeDtypeStruct((rows, n), F32),
        compiler_params=_params("arbitrary"),
        name="ada_mod",
    )(c_pad, w_ada, b_ada)


def _norm_mod_kernel(x_ref, g_ref, sc_ref, sh_ref, o_ref):
    x = x_ref[...]
    y = x * lax.rsqrt(jnp.mean(x * x, axis=-1, keepdims=True) + NORM_EPS) * g_ref[...]
    o_ref[...] = (y * (1.0 + sc_ref[...]) + sh_ref[...]).astype(o_ref.dtype)


def _norm_mod(x2d, gain, mod4, sc_idx, sh_idx, seq, tm=512):
    m, d = x2d.shape
    per_batch = seq // tm
    return pl.pallas_call(
        _norm_mod_kernel,
        grid=(m // tm,),
        in_specs=[pl.BlockSpec((tm, d), lambda i: (i, 0)),
                  pl.BlockSpec((1, d), lambda i: (0, 0)),
                  pl.BlockSpec((None, None, 1, d), lambda i: (i // per_batch, sc_idx, 0, 0)),
                  pl.BlockSpec((None, None, 1, d), lambda i: (i // per_batch, sh_idx, 0, 0))],
        out_specs=pl.BlockSpec((tm, d), lambda i: (i, 0)),
        out_shape=jax.ShapeDtypeStruct((m, d), BF16),
        compiler_params=_params("arbitrary"),
        name="norm_mod",
    )(x2d, gain, mod4, mod4)


def _matmul_kernel(a_ref, w_ref, o_ref):
    o_ref[...] = jnp.dot(a_ref[...], w_ref[...].astype(BF16),
                         preferred_element_type=F32).astype(o_ref.dtype)


def _matmul(a, w, tm=1024, tn=512):
    m, k = a.shape
    n = w.shape[1]
    return pl.pallas_call(
        _matmul_kernel,
        grid=(m // tm, n // tn),
        in_specs=[pl.BlockSpec((tm, k), lambda i, j: (i, 0)),
                  pl.BlockSpec((k, tn), lambda i, j: (0, j))],
        out_specs=pl.BlockSpec((tm, tn), lambda i, j: (i, j)),
        out_shape=jax.ShapeDtypeStruct((m, n), BF16),
        compiler_params=_params("arbitrary", "arbitrary"),
        name="in_proj",
    )(a, w)


def _attn_kernel(q_ref, k_ref, v_ref, cos_ref, sa_ref, sb_ref, lq1_ref, lk1_ref,
                 lq2_ref, lk2_ref, g_ref, *rest, tq, n_cast):
    cast_src, o_ref, cast_dst = rest[:n_cast], rest[n_cast], rest[n_cast + 1:2 * n_cast + 1]
    q1_s, q2_s, k_s = rest[2 * n_cast + 1:]
    for src, dst in zip(cast_src, cast_dst):
        dst[...] = src[...].astype(dst.dtype)
    seq, width = q_ref.shape
    half = ATTN_HEAD_DIM // 2
    cos, sa, sb = cos_ref[...], sa_ref[...], sb_ref[...]

    def rope(t):
        return (t * cos + pltpu.roll(t, width - half, 1) * sa
                + pltpu.roll(t, half, 1) * sb)

    lane = lax.broadcasted_iota(jnp.int32, (seq, width), 1)
    first = lane < ATTN_HEAD_DIM
    q = rope(q_ref[...].astype(F32)) * (ATTN_HEAD_DIM ** -0.5)
    q1_s[...] = jnp.where(first, q, 0.0).astype(BF16)
    q2_s[...] = jnp.where(first, 0.0, q).astype(BF16)
    k_s[...] = rope(k_ref[...].astype(F32)).astype(BF16)

    lam = (jnp.exp(jnp.sum(lq1_ref[...] * lk1_ref[...], keepdims=True))
           - jnp.exp(jnp.sum(lq2_ref[...] * lk2_ref[...], keepdims=True))
           + LAMBDA_INIT)
    neg = jnp.finfo(F32).min
    gain = g_ref[...] * (1.0 - LAMBDA_INIT)

    for qi in range(seq // tq):
        r0 = qi * tq
        kv_len = r0 + tq
        kb = k_s[0:kv_len, :]
        row = r0 + lax.broadcasted_iota(jnp.int32, (tq, kv_len), 0)
        col = lax.broadcasted_iota(jnp.int32, (tq, kv_len), 1)
        mask = col <= row

        def probs(qz):
            s = lax.dot_general(qz, kb, (((1,), (1,)), ((), ())),
                                preferred_element_type=F32)
            s = jnp.where(mask, s, neg)
            e = jnp.exp(s - jnp.max(s, axis=-1, keepdims=True))
            return e * (1.0 / jnp.sum(e, axis=-1, keepdims=True))

        p = probs(q1_s[r0:r0 + tq, :]) - lam * probs(q2_s[r0:r0 + tq, :])
        o = jnp.dot(p.astype(BF16), v_ref[0:kv_len, :], preferred_element_type=F32)
        o = o * lax.rsqrt(jnp.mean(o * o, axis=-1, keepdims=True) + NORM_EPS) * gain
        o_ref[r0:r0 + tq, :] = o.astype(o_ref.dtype)


def _split_blocks(shape, n_steps):
    rows, cols = shape
    for rb in range(n_steps, 0, -1):
        cb = n_steps // rb
        if (rb * cb == n_steps and rows % (rb * BF16_SUBLANES) == 0
                and cols % (cb * LANES) == 0):
            return rb, cb
    raise ValueError(f"cannot split {shape} into {n_steps} blocks")


def _attention(proj, tables, lams, subln_g, cast_weights, batch, seq, n_heads, mix_width,
               tq=256):
    dv = ATTN_V_DIM
    cos, sa, sb = tables
    full = lambda b, h: (0, 0)
    vec = pl.BlockSpec((1, ATTN_HEAD_DIM), full)
    cast_specs = []
    for w in cast_weights:
        rb, cb = _split_blocks(w.shape, batch * n_heads)
        cast_specs.append(pl.BlockSpec(
            (w.shape[0] // rb, w.shape[1] // cb),
            lambda b, h, cb=cb: ((b * n_heads + h) // cb, (b * n_heads + h) % cb)))
    n_cast = len(cast_weights)
    outs = pl.pallas_call(
        functools.partial(_attn_kernel, tq=tq, n_cast=n_cast),
        grid=(batch, n_heads),
        in_specs=[pl.BlockSpec((seq, dv), lambda b, h: (b, h)),
                  pl.BlockSpec((seq, dv), lambda b, h: (b, n_heads + h)),
                  pl.BlockSpec((seq, dv), lambda b, h: (b, 2 * n_heads + h)),
                  pl.BlockSpec((seq, dv), full), pl.BlockSpec((seq, dv), full),
                  pl.BlockSpec((seq, dv), full),
                  vec, vec, vec, vec,
                  pl.BlockSpec((1, dv), full)] + cast_specs,
        out_specs=[pl.BlockSpec((seq, dv), lambda b, h: (b, h))] + cast_specs,
        out_shape=[jax.ShapeDtypeStruct((batch * seq, mix_width), BF16)]
        + [jax.ShapeDtypeStruct(w.shape, BF16) for w in cast_weights],
        scratch_shapes=[pltpu.VMEM((seq, dv), BF16)] * 3,
        compiler_params=_params("arbitrary", "arbitrary"),
        name="diff_attn",
    )(proj, proj, proj, cos, sa, sb, *lams, subln_g, *cast_weights)
    return outs[0], outs[1:]


def _conv_kernel(mix_ref, bg_ref, cg_ref, xg_ref, w_ref, o_ref):
    del mix_ref
    u = cg_ref[...].astype(F32) * xg_ref[...].astype(F32)
    w = w_ref[...]
    row = lax.broadcasted_iota(jnp.int32, u.shape, 0)
    y = w[CONV_K - 1:CONV_K, :] * u
    for shift in range(1, CONV_K):
        shifted = jnp.where(row >= shift, pltpu.roll(u, shift, 0), 0.0)
        y = y + w[CONV_K - 1 - shift:CONV_K - shift, :] * shifted
    o_ref[...] = (bg_ref[...].astype(F32) * y).astype(o_ref.dtype)


def _conv(mix, proj, conv_w, batch, seq, attn_width, conv_width, tc=512):
    qkv_blocks = 3 * attn_width // tc
    cblocks = conv_width // tc
    return pl.pallas_call(
        _conv_kernel,
        grid=(batch, cblocks),
        in_specs=[pl.BlockSpec(memory_space=pl.ANY),
                  pl.BlockSpec((seq, tc), lambda b, j: (b, qkv_blocks + j)),
                  pl.BlockSpec((seq, tc), lambda b, j: (b, qkv_blocks + cblocks + j)),
                  pl.BlockSpec((seq, tc), lambda b, j: (b, qkv_blocks + 2 * cblocks + j)),
                  pl.BlockSpec((CONV_K, tc), lambda b, j: (0, j))],
        out_specs=pl.BlockSpec((seq, tc), lambda b, j: (b, attn_width // tc + j)),
        out_shape=jax.ShapeDtypeStruct(mix.shape, mix.dtype),
        input_output_aliases={0: 0},
        compiler_params=_params("arbitrary", "arbitrary"),
        name="gated_conv",
    )(mix, proj, proj, proj, conv_w)


def _out_proj_kernel(a_ref, w_ref, x_ref, g_ref, o_ref):
    acc = jnp.dot(a_ref[...], w_ref[...], preferred_element_type=F32)
    o_ref[...] = x_ref[...] + g_ref[...] * acc


def _out_proj(mix, w_out, x2d, mod4, gate_idx, seq, tm=1024, tn=512):
    m, k = mix.shape
    n = w_out.shape[1]
    per_batch = seq // tm
    return pl.pallas_call(
        _out_proj_kernel,
        grid=(m // tm, n // tn),
        in_specs=[pl.BlockSpec((tm, k), lambda i, j: (i, 0)),
                  pl.BlockSpec((k, tn), lambda i, j: (0, j)),
                  pl.BlockSpec((tm, tn), lambda i, j: (i, j)),
                  pl.BlockSpec((None, None, 1, tn),
                               lambda i, j: (i // per_batch, gate_idx, 0, j))],
        out_specs=pl.BlockSpec((tm, tn), lambda i, j: (i, j)),
        out_shape=jax.ShapeDtypeStruct((m, n), F32),
        compiler_params=_params("arbitrary", "arbitrary"),
        name="out_proj",
    )(mix, w_out, x2d, mod4)


def _ffn_kernel(h_ref, wg_ref, wu_ref, wd_ref, o_ref, acc_ref):
    f = pl.program_id(1)

    @pl.when(f == 0)
    def _():
        acc_ref[...] = jnp.zeros_like(acc_ref)

    h = h_ref[...]
    g = jnp.dot(h, wg_ref[...], preferred_element_type=F32)
    u = jnp.dot(h, wu_ref[...], preferred_element_type=F32)
    a = (_silu(g) * u).astype(BF16)
    acc_ref[...] += jnp.dot(a, wd_ref[...], preferred_element_type=F32)

    @pl.when(f == pl.num_programs(1) - 1)
    def _():
        o_ref[...] = acc_ref[...].astype(o_ref.dtype)


def _ffn(h, w_gate, w_up, w_down, tm=512, tf=256):
    m, d = h.shape
    d_ff = w_gate.shape[1]
    return pl.pallas_call(
        _ffn_kernel,
        grid=(m // tm, d_ff // tf),
        in_specs=[pl.BlockSpec((tm, d), lambda i, f: (i, 0)),
                  pl.BlockSpec((d, tf), lambda i, f: (0, f)),
                  pl.BlockSpec((d, tf), lambda i, f: (0, f)),
                  pl.BlockSpec((tf, d), lambda i, f: (f, 0))],
        out_specs=pl.BlockSpec((tm, d), lambda i, f: (i, 0)),
        out_shape=jax.ShapeDtypeStruct((m, d), BF16),
        scratch_shapes=[pltpu.VMEM((tm, d), F32)],
        compiler_params=_params("arbitrary", "arbitrary"),
        name="swiglu",
    )(h, w_gate, w_up, w_down)


def _final_kernel(x_ref, f_ref, g2_ref, gain_ref, o_ref):
    x = x_ref[...] + g2_ref[...] * f_ref[...].astype(F32)
    o_ref[...] = (x * lax.rsqrt(jnp.mean(x * x, axis=-1, keepdims=True) + NORM_EPS)
                  * gain_ref[...])


def _final(x1, ffn, mod4, gate_idx, final_g, seq, tm=512):
    m, d = x1.shape
    per_batch = seq // tm
    return pl.pallas_call(
        _final_kernel,
        grid=(m // tm,),
        in_specs=[pl.BlockSpec((tm, d), lambda i: (i, 0)),
                  pl.BlockSpec((tm, d), lambda i: (i, 0)),
                  pl.BlockSpec((None, None, 1, d), lambda i: (i // per_batch, gate_idx, 0, 0)),
                  pl.BlockSpec((1, d), lambda i: (0, 0))],
        out_specs=pl.BlockSpec((tm, d), lambda i: (i, 0)),
        out_shape=jax.ShapeDtypeStruct((m, d), F32),
        compiler_params=_params("arbitrary"),
        name="final_norm",
    )(x1, ffn, mod4, final_g)


def _rope_tables(seq):
    half = ATTN_HEAD_DIM // 2
    inv_freq = ROPE_THETA ** (-jnp.arange(half, dtype=F32) / half)
    ang = jnp.arange(seq, dtype=F32)[:, None] * inv_freq[None, :]
    cos, sin, zero = jnp.cos(ang), jnp.sin(ang), jnp.zeros_like(ang)
    reps = ATTN_V_DIM // ATTN_HEAD_DIM
    cos_t = jnp.tile(cos, (1, 2 * reps))
    sin_lo = jnp.tile(jnp.concatenate([-sin, zero], axis=1), (1, reps))
    sin_hi = jnp.tile(jnp.concatenate([zero, sin], axis=1), (1, reps))
    return cos_t, sin_lo, sin_hi


def kernel(x, c, w_ada, b_ada, norm1_g, w_in, lambda_q1, lambda_k1, lambda_q2, lambda_k2,
           subln_g, conv_w, w_out, norm2_g, w_gate, w_up, w_down, final_g):
    batch, seq, d = x.shape
    depth = w_ada.shape[0]
    assert depth == 1
    conv_width = conv_w.shape[-1]
    mix_width = w_out.shape[1]
    attn_width = mix_width - conv_width
    n_heads = attn_width // ATTN_V_DIM
    x2d = x.reshape(batch * seq, d)
    tables = _rope_tables(seq)

    for l in range(depth):
        c_pad = jnp.pad(c, ((0, 8 - batch), (0, 0)))
        mod = _ada(c_pad, w_ada[l], b_ada[l][None, :])
        mod4 = mod[:batch].reshape(batch, N_MOD, 1, d)

        h = _norm_mod(x2d, norm1_g[l][None, :], mod4, 1, 0, seq)
        proj = _matmul(h, w_in[l])
        lams = [v[l][None, :] for v in (lambda_q1, lambda_k1, lambda_q2, lambda_k2)]
        mix, (w_out_b, w_gate_b, w_up_b, w_down_b) = _attention(
            proj, tables, lams, subln_g[l][None, :],
            (w_out[l], w_gate[l], w_up[l], w_down[l]), batch, seq, n_heads, mix_width)
        mix = _conv(mix, proj, conv_w[l], batch, seq, attn_width, conv_width)
        x2d = _out_proj(mix, w_out_b, x2d, mod4, 2, seq)

        h = _norm_mod(x2d, norm2_g[l][None, :], mod4, 4, 3, seq)
        ffn = _ffn(h, w_gate_b, w_up_b, w_down_b)
        x2d = _final(x2d, ffn, mod4, 5, final_g[None, :], seq)

    return x2d.reshape(batch, seq, d)
```

```python
import functools
import math

import jax
import jax.numpy as jnp
from jax import lax
from jax.experimental import pallas as pl
from jax.experimental.pallas import tpu as pltpu

F32 = jnp.float32
BF16 = jnp.bfloat16

ATTN_HEAD_DIM = 64
ATTN_V_DIM = 2 * ATTN_HEAD_DIM
CONV_K = 3
ROPE_THETA = 10000.0
NORM_EPS = 1e-6
N_MOD = 6
LAMBDA_INIT = 0.8 - 0.6 * math.exp(-0.3 * 0)
LOG2_E = math.log2(math.e)

VMEM_LIMIT_BYTES = 60 * 1024 * 1024
LANES = 128
BF16_SUBLANES = 16


def _params(*semantics):
    return pltpu.CompilerParams(dimension_semantics=semantics,
                                vmem_limit_bytes=VMEM_LIMIT_BYTES)


def _silu(v):
    return v * jax.nn.sigmoid(v)


def _ada_kernel(c_ref, w_ref, b_ref, o_ref):
    c_act = _silu(c_ref[...]).astype(BF16)
    o_ref[...] = jnp.dot(c_act, w_ref[...].astype(BF16),
                         preferred_element_type=F32) + b_ref[...]


def _ada(c_pad, w_ada, b_ada, tn=512):
    rows, d = c_pad.shape
    n = w_ada.shape[1]
    return pl.pallas_call(
        _ada_kernel,
        grid=(n // tn,),
        in_specs=[pl.BlockSpec((rows, d), lambda j: (0, 0)),
                  pl.BlockSpec((d, tn), lambda j: (0, j)),
                  pl.BlockSpec((1, tn), lambda j: (0, j))],
        out_specs=pl.BlockSpec((rows, tn), lambda j: (0, j)),
        out_shape=jax.ShapeDtypeStruct((rows, n), F32),
        compiler_params=_params("arbitrary"),
        name="ada_mod",
    )(c_pad, w_ada, b_ada)


def _norm_mod_kernel(x_ref, g_ref, sc_ref, sh_ref, o_ref):
    x = x_ref[...]
    y = x * lax.rsqrt(jnp.mean(x * x, axis=-1, keepdims=True) + NORM_EPS) * g_ref[...]
    o_ref[...] = (y * (1.0 + sc_ref[...]) + sh_ref[...]).astype(o_ref.dtype)


def _norm_mod(x2d, gain, mod4, sc_idx, sh_idx, seq, tm=512):
    m, d = x2d.shape
    per_batch = seq // tm
    return pl.pallas_call(
        _norm_mod_kernel,
        grid=(m // tm,),
        in_specs=[pl.BlockSpec((tm, d), lambda i: (i, 0)),
                  pl.BlockSpec((1, d), lambda i: (0, 0)),
                  pl.BlockSpec((None, None, 1, d), lambda i: (i // per_batch, sc_idx, 0, 0)),
                  pl.BlockSpec((None, None, 1, d), lambda i: (i // per_batch, sh_idx, 0, 0))],
        out_specs=pl.BlockSpec((tm, d), lambda i: (i, 0)),
        out_shape=jax.ShapeDtypeStruct((m, d), BF16),
        compiler_params=_params("arbitrary"),
        name="norm_mod",
    )(x2d, gain, mod4, mod4)


def _matmul_kernel(a_ref, w_ref, o_ref):
    o_ref[...] = jnp.dot(a_ref[...], w_ref[...].astype(BF16),
                         preferred_element_type=F32).astype(o_ref.dtype)


def _matmul(a, w, tm=1024, tn=512):
    m, k = a.shape
    n = w.shape[1]
    return pl.pallas_call(
        _matmul_kernel,
        grid=(m // tm, n // tn),
        in_specs=[pl.BlockSpec((tm, k), lambda i, j: (i, 0)),
                  pl.BlockSpec((k, tn), lambda i, j: (0, j))],
        out_specs=pl.BlockSpec((tm, tn), lambda i, j: (i, j)),
        out_shape=jax.ShapeDtypeStruct((m, n), BF16),
        compiler_params=_params("arbitrary", "arbitrary"),
        name="in_proj",
    )(a, w)


def _reduce_keys(t, op, group=64):
    keys, queries = t.shape
    partial = op(t.reshape(keys // group, group, queries), axis=0)
    return op(partial, axis=0, keepdims=True)


def _attn_kernel(q_ref, k_ref, v_ref, cos_ref, sa_ref, sb_ref, lq1_ref, lk1_ref,
                 lq2_ref, lk2_ref, g_ref, *rest, tq, n_cast):
    cast_src, o_ref, cast_dst = rest[:n_cast], rest[n_cast], rest[n_cast + 1:2 * n_cast + 1]
    q1_s, q2_s, k_s, vt_s = rest[2 * n_cast + 1:]
    for src, dst in zip(cast_src, cast_dst):
        dst[...] = src[...].astype(dst.dtype)
    seq, width = q_ref.shape
    half = ATTN_HEAD_DIM // 2
    cos, sa, sb = cos_ref[...], sa_ref[...], sb_ref[...]

    def rope(t):
        return (t * cos + pltpu.roll(t, width - half, 1) * sa
                + pltpu.roll(t, half, 1) * sb)

    lane = lax.broadcasted_iota(jnp.int32, (seq, width), 1)
    first = lane < ATTN_HEAD_DIM
    q = rope(q_ref[...].astype(F32)) * (ATTN_HEAD_DIM ** -0.5 * LOG2_E)
    q1_s[...] = jnp.where(first, q, 0.0).astype(BF16)
    q2_s[...] = jnp.where(first, 0.0, q).astype(BF16)
    k_s[...] = rope(k_ref[...].astype(F32)).astype(BF16)
    vt_s[...] = v_ref[...].astype(F32).T.astype(BF16)

    lam = (jnp.exp(jnp.sum(lq1_ref[...] * lk1_ref[...], keepdims=True))
           - jnp.exp(jnp.sum(lq2_ref[...] * lk2_ref[...], keepdims=True))
           + LAMBDA_INIT)
    neg = jnp.finfo(F32).min
    gain = g_ref[...] * (1.0 - LAMBDA_INIT)

    for qi in range(seq // tq):
        r0 = qi * tq
        kv_len = r0 + tq
        qz = jnp.concatenate([q1_s[r0:r0 + tq, :], q2_s[r0:r0 + tq, :]], axis=0)
        key = lax.broadcasted_iota(jnp.int32, (kv_len, 2 * tq), 0)
        col = lax.broadcasted_iota(jnp.int32, (kv_len, 2 * tq), 1)
        query = r0 + jnp.where(col < tq, col, col - tq)
        s = lax.dot_general(k_s[0:kv_len, :], qz, (((1,), (1,)), ((), ())),
                            preferred_element_type=F32)
        s = jnp.where(key <= query, s, neg)
        e = jnp.exp2(s - _reduce_keys(s, jnp.max))
        inv = 1.0 / _reduce_keys(e, jnp.sum)
        ov = jnp.dot(vt_s[:, 0:kv_len], e.astype(BF16),
                     preferred_element_type=F32) * inv
        o = ov[:, :tq] - lam * ov[:, tq:]
        o = o * lax.rsqrt(jnp.mean(o * o, axis=0, keepdims=True) + NORM_EPS)
        o_ref[r0:r0 + tq, :] = (o.T * gain).astype(o_ref.dtype)


def _split_blocks(shape, n_steps):
    rows, cols = shape
    for rb in range(n_steps, 0, -1):
        cb = n_steps // rb
        if (rb * cb == n_steps and rows % (rb * BF16_SUBLANES) == 0
                and cols % (cb * LANES) == 0):
            return rb, cb
    raise ValueError(f"cannot split {shape} into {n_steps} blocks")


def _attention(proj, tables, lams, subln_g, cast_weights, batch, seq, n_heads, mix_width,
               tq=512):
    dv = ATTN_V_DIM
    cos, sa, sb = tables
    full = lambda b, h: (0, 0)
    vec = pl.BlockSpec((1, ATTN_HEAD_DIM), full)
    cast_specs = []
    for w in cast_weights:
        rb, cb = _split_blocks(w.shape, batch * n_heads)
        cast_specs.append(pl.BlockSpec(
            (w.shape[0] // rb, w.shape[1] // cb),
            lambda b, h, cb=cb: ((b * n_heads + h) // cb, (b * n_heads + h) % cb)))
    n_cast = len(cast_weights)
    outs = pl.pallas_call(
        functools.partial(_attn_kernel, tq=tq, n_cast=n_cast),
        grid=(batch, n_heads),
        in_specs=[pl.BlockSpec((seq, dv), lambda b, h: (b, h)),
                  pl.BlockSpec((seq, dv), lambda b, h: (b, n_heads + h)),
                  pl.BlockSpec((seq, dv), lambda b, h: (b, 2 * n_heads + h)),
                  pl.BlockSpec((seq, dv), full), pl.BlockSpec((seq, dv), full),
                  pl.BlockSpec((seq, dv), full),
                  vec, vec, vec, vec,
                  pl.BlockSpec((1, dv), full)] + cast_specs,
        out_specs=[pl.BlockSpec((seq, dv), lambda b, h: (b, h))] + cast_specs,
        out_shape=[jax.ShapeDtypeStruct((batch * seq, mix_width), BF16)]
        + [jax.ShapeDtypeStruct(w.shape, BF16) for w in cast_weights],
        scratch_shapes=[pltpu.VMEM((seq, dv), BF16)] * 3 + [pltpu.VMEM((dv, seq), BF16)],
        compiler_params=_params("arbitrary", "arbitrary"),
        name="diff_attn",
    )(proj, proj, proj, cos, sa, sb, *lams, subln_g, *cast_weights)
    return outs[0], outs[1:]


def _conv_kernel(mix_ref, bg_ref, cg_ref, xg_ref, w_ref, o_ref):
    del mix_ref
    u = cg_ref[...].astype(F32) * xg_ref[...].astype(F32)
    w = w_ref[...]
    row = lax.broadcasted_iota(jnp.int32, u.shape, 0)
    y = w[CONV_K - 1:CONV_K, :] * u
    for shift in range(1, CONV_K):
        shifted = jnp.where(row >= shift, pltpu.roll(u, shift, 0), 0.0)
        y = y + w[CONV_K - 1 - shift:CONV_K - shift, :] * shifted
    o_ref[...] = (bg_ref[...].astype(F32) * y).astype(o_ref.dtype)


def _conv(mix, proj, conv_w, batch, seq, attn_width, conv_width, tc=512):
    qkv_blocks = 3 * attn_width // tc
    cblocks = conv_width // tc
    return pl.pallas_call(
        _conv_kernel,
        grid=(batch, cblocks),
        in_specs=[pl.BlockSpec(memory_space=pl.ANY),
                  pl.BlockSpec((seq, tc), lambda b, j: (b, qkv_blocks + j)),
                  pl.BlockSpec((seq, tc), lambda b, j: (b, qkv_blocks + cblocks + j)),
                  pl.BlockSpec((seq, tc), lambda b, j: (b, qkv_blocks + 2 * cblocks + j)),
                  pl.BlockSpec((CONV_K, tc), lambda b, j: (0, j))],
        out_specs=pl.BlockSpec((seq, tc), lambda b, j: (b, attn_width // tc + j)),
        out_shape=jax.ShapeDtypeStruct(mix.shape, mix.dtype),
        input_output_aliases={0: 0},
        compiler_params=_params("arbitrary", "arbitrary"),
        name="gated_conv",
    )(mix, proj, proj, proj, conv_w)


def _out_proj_kernel(a_ref, w_ref, x_ref, g_ref, o_ref):
    acc = jnp.dot(a_ref[...], w_ref[...], preferred_element_type=F32)
    o_ref[...] = x_ref[...] + g_ref[...] * acc


def _out_proj(mix, w_out, x2d, mod4, gate_idx, seq, tm=1024, tn=1024):
    m, k = mix.shape
    n = w_out.shape[1]
    per_batch = seq // tm
    return pl.pallas_call(
        _out_proj_kernel,
        grid=(m // tm, n // tn),
        in_specs=[pl.BlockSpec((tm, k), lambda i, j: (i, 0)),
                  pl.BlockSpec((k, tn), lambda i, j: (0, j)),
                  pl.BlockSpec((tm, tn), lambda i, j: (i, j)),
                  pl.BlockSpec((None, None, 1, tn),
                               lambda i, j: (i // per_batch, gate_idx, 0, j))],
        out_specs=pl.BlockSpec((tm, tn), lambda i, j: (i, j)),
        out_shape=jax.ShapeDtypeStruct((m, n), F32),
        compiler_params=_params("arbitrary", "arbitrary"),
        name="out_proj",
    )(mix, w_out, x2d, mod4)


def _rms(x):
    return x * lax.rsqrt(jnp.mean(x * x, axis=-1, keepdims=True) + NORM_EPS)


def _ffn_kernel(x_ref, gain_ref, sc_ref, sh_ref, gate_ref, fgain_ref,
                wg_ref, wu_ref, wd_ref, o_ref, h_s):
    f = pl.program_id(1)

    @pl.when(f == 0)
    def _():
        y = _rms(x_ref[...]) * gain_ref[...]
        h_s[...] = (y * (1.0 + sc_ref[...]) + sh_ref[...]).astype(h_s.dtype)
        o_ref[...] = jnp.zeros_like(o_ref)

    h = h_s[...]
    g = jnp.dot(h, wg_ref[...], preferred_element_type=F32)
    u = jnp.dot(h, wu_ref[...], preferred_element_type=F32)
    a = (_silu(g) * u).astype(BF16)
    o_ref[...] += jnp.dot(a, wd_ref[...], preferred_element_type=F32)

    @pl.when(f == pl.num_programs(1) - 1)
    def _():
        o_ref[...] = _rms(x_ref[...] + gate_ref[...] * o_ref[...]) * fgain_ref[...]


def _ffn(x1, gain, mod4, sc_idx, sh_idx, gate_idx, final_g, w_gate, w_up, w_down, seq,
         tm=512, tf=256):
    m, d = x1.shape
    d_ff = w_gate.shape[1]
    per_batch = seq // tm
    row = pl.BlockSpec((1, d), lambda i, f: (0, 0))

    def mod_spec(idx):
        return pl.BlockSpec((None, None, 1, d), lambda i, f: (i // per_batch, idx, 0, 0))

    return pl.pallas_call(
        _ffn_kernel,
        grid=(m // tm, d_ff // tf),
        in_specs=[pl.BlockSpec((tm, d), lambda i, f: (i, 0)),
                  row, mod_spec(sc_idx), mod_spec(sh_idx), mod_spec(gate_idx), row,
                  pl.BlockSpec((d, tf), lambda i, f: (0, f)),
                  pl.BlockSpec((d, tf), lambda i, f: (0, f)),
                  pl.BlockSpec((tf, d), lambda i, f: (f, 0))],
        out_specs=pl.BlockSpec((tm, d), lambda i, f: (i, 0)),
        out_shape=jax.ShapeDtypeStruct((m, d), F32),
        scratch_shapes=[pltpu.VMEM((tm, d), BF16)],
        compiler_params=_params("arbitrary", "arbitrary"),
        name="swiglu",
    )(x1, gain, mod4, mod4, mod4, final_g, w_gate, w_up, w_down)


def _rope_tables(seq):
    half = ATTN_HEAD_DIM // 2
    inv_freq = ROPE_THETA ** (-jnp.arange(half, dtype=F32) / half)
    ang = jnp.arange(seq, dtype=F32)[:, None] * inv_freq[None, :]
    cos, sin, zero = jnp.cos(ang), jnp.sin(ang), jnp.zeros_like(ang)
    reps = ATTN_V_DIM // ATTN_HEAD_DIM
    cos_t = jnp.tile(cos, (1, 2 * reps))
    sin_lo = jnp.tile(jnp.concatenate([-sin, zero], axis=1), (1, reps))
    sin_hi = jnp.tile(jnp.concatenate([zero, sin], axis=1), (1, reps))
    return cos_t, sin_lo, sin_hi


def kernel(x, c, w_ada, b_ada, norm1_g, w_in, lambda_q1, lambda_k1, lambda_q2, lambda_k2,
           subln_g, conv_w, w_out, norm2_g, w_gate, w_up, w_down, final_g):
    batch, seq, d = x.shape
    depth = w_ada.shape[0]
    assert depth == 1
    conv_width = conv_w.shape[-1]
    mix_width = w_out.shape[1]
    attn_width = mix_width - conv_width
    n_heads = attn_width // ATTN_V_DIM
    x2d = x.reshape(batch * seq, d)
    tables = _rope_tables(seq)

    for l in range(depth):
        c_pad = jnp.pad(c, ((0, 8 - batch), (0, 0)))
        mod = _ada(c_pad, w_ada[l], b_ada[l][None, :])
        mod4 = mod[:batch].reshape(batch, N_MOD, 1, d)

        h = _norm_mod(x2d, norm1_g[l][None, :], mod4, 1, 0, seq)
        proj = _matmul(h, w_in[l])
        lams = [v[l][None, :] for v in (lambda_q1, lambda_k1, lambda_q2, lambda_k2)]
        mix, (w_out_b, w_gate_b, w_up_b, w_down_b) = _attention(
            proj, tables, lams, subln_g[l][None, :],
            (w_out[l], w_gate[l], w_up[l], w_down[l]), batch, seq, n_heads, mix_width)
        mix = _conv(mix, proj, conv_w[l], batch, seq, attn_width, conv_width)
        x2d = _out_proj(mix, w_out_b, x2d, mod4, 2, seq)

        x2d = _ffn(x2d, norm2_g[l][None, :], mod4, 4, 3, 5, final_g[None, :],
                   w_gate_b, w_up_b, w_down_b, seq)

    return x2d.reshape(batch, seq, d)
```

```python
import functools
import math

import jax
import jax.numpy as jnp
from jax import lax
from jax.experimental import pallas as pl
from jax.experimental.pallas import tpu as pltpu

F32 = jnp.float32
BF16 = jnp.bfloat16

ATTN_HEAD_DIM = 64
ATTN_V_DIM = 2 * ATTN_HEAD_DIM
CONV_K = 3
ROPE_THETA = 10000.0
NORM_EPS = 1e-6
N_MOD = 6
LAMBDA_INIT = 0.8 - 0.6 * math.exp(-0.3 * 0)
LOG2_E = math.log2(math.e)

VMEM_LIMIT_BYTES = 60 * 1024 * 1024
LANES = 128
BF16_SUBLANES = 16


def _params(*semantics):
    return pltpu.CompilerParams(dimension_semantics=semantics,
                                vmem_limit_bytes=VMEM_LIMIT_BYTES)


def _silu(v):
    return v * jax.nn.sigmoid(v)


def _ada_kernel(c_ref, w_ref, b_ref, o_ref):
    c_act = _silu(c_ref[...]).astype(BF16)
    o_ref[...] = jnp.dot(c_act, w_ref[...].astype(BF16),
                         preferred_element_type=F32) + b_ref[...]


def _ada(c_pad, w_ada, b_ada, tn=512):
    rows, d = c_pad.shape
    n = w_ada.shape[1]
    return pl.pallas_call(
        _ada_kernel,
        grid=(n // tn,),
        in_specs=[pl.BlockSpec((rows, d), lambda j: (0, 0)),
                  pl.BlockSpec((d, tn), lambda j: (0, j)),
                  pl.BlockSpec((1, tn), lambda j: (0, j))],
        out_specs=pl.BlockSpec((rows, tn), lambda j: (0, j)),
        out_shape=jax.ShapeDtypeStruct((rows, n), F32),
        compiler_params=_params("arbitrary"),
        name="ada_mod",
    )(c_pad, w_ada, b_ada)


def _norm_mod_kernel(x_ref, g_ref, sc_ref, sh_ref, o_ref):
    x = x_ref[...]
    y = x * lax.rsqrt(jnp.mean(x * x, axis=-1, keepdims=True) + NORM_EPS) * g_ref[...]
    o_ref[...] = (y * (1.0 + sc_ref[...]) + sh_ref[...]).astype(o_ref.dtype)


def _norm_mod(x2d, gain, mod4, sc_idx, sh_idx, seq, tm=512):
    m, d = x2d.shape
    per_batch = seq // tm
    return pl.pallas_call(
        _norm_mod_kernel,
        grid=(m // tm,),
        in_specs=[pl.BlockSpec((tm, d), lambda i: (i, 0)),
                  pl.BlockSpec((1, d), lambda i: (0, 0)),
                  pl.BlockSpec((None, None, 1, d), lambda i: (i // per_batch, sc_idx, 0, 0)),
                  pl.BlockSpec((None, None, 1, d), lambda i: (i // per_batch, sh_idx, 0, 0))],
        out_specs=pl.BlockSpec((tm, d), lambda i: (i, 0)),
        out_shape=jax.ShapeDtypeStruct((m, d), BF16),
        compiler_params=_params("arbitrary"),
        name="norm_mod",
    )(x2d, gain, mod4, mod4)


def _matmul_kernel(a_ref, w_ref, o_ref):
    o_ref[...] = jnp.dot(a_ref[...], w_ref[...].astype(BF16),
                         preferred_element_type=F32).astype(o_ref.dtype)


def _matmul(a, w, tm=1024, tn=512):
    m, k = a.shape
    n = w.shape[1]
    return pl.pallas_call(
        _matmul_kernel,
        grid=(m // tm, n // tn),
        in_specs=[pl.BlockSpec((tm, k), lambda i, j: (i, 0)),
                  pl.BlockSpec((k, tn), lambda i, j: (0, j))],
        out_specs=pl.BlockSpec((tm, tn), lambda i, j: (i, j)),
        out_shape=jax.ShapeDtypeStruct((m, n), BF16),
        compiler_params=_params("arbitrary", "arbitrary"),
        name="in_proj",
    )(a, w)


def _reduce_keys(t, op, group=64):
    keys, queries = t.shape
    partial = op(t.reshape(keys // group, group, queries), axis=0)
    return op(partial, axis=0, keepdims=True)


def _attn_kernel(q_ref, k_ref, v_ref, cos_ref, sa_ref, sb_ref, lq1_ref, lk1_ref,
                 lq2_ref, lk2_ref, g_ref, *rest, tq, n_cast):
    cast_src, o_ref, cast_dst = rest[:n_cast], rest[n_cast], rest[n_cast + 1:2 * n_cast + 1]
    q1_s, q2_s, k_s, vt_s = rest[2 * n_cast + 1:]
    for src, dst in zip(cast_src, cast_dst):
        dst[...] = src[...].astype(dst.dtype)
    seq, width = q_ref.shape
    half = ATTN_HEAD_DIM // 2
    cos, sa, sb = cos_ref[...], sa_ref[...], sb_ref[...]

    def rope(t):
        return (t * cos + pltpu.roll(t, width - half, 1) * sa
                + pltpu.roll(t, half, 1) * sb)

    lane = lax.broadcasted_iota(jnp.int32, (seq, width), 1)
    first = lane < ATTN_HEAD_DIM
    q = rope(q_ref[...].astype(F32)) * (ATTN_HEAD_DIM ** -0.5 * LOG2_E)
    q1_s[...] = jnp.where(first, q, 0.0).astype(BF16)
    q2_s[...] = jnp.where(first, 0.0, q).astype(BF16)
    k_s[...] = rope(k_ref[...].astype(F32)).astype(BF16)
    vt_s[...] = v_ref[...].astype(F32).T.astype(BF16)

    lam = (jnp.exp(jnp.sum(lq1_ref[...] * lk1_ref[...], keepdims=True))
           - jnp.exp(jnp.sum(lq2_ref[...] * lk2_ref[...], keepdims=True))
           + LAMBDA_INIT)
    neg = jnp.finfo(F32).min
    gain = g_ref[...] * (1.0 - LAMBDA_INIT)

    for qi in range(seq // tq):
        r0 = qi * tq
        kv_len = r0 + tq
        qz = jnp.concatenate([q1_s[r0:r0 + tq, :], q2_s[r0:r0 + tq, :]], axis=0)
        key = lax.broadcasted_iota(jnp.int32, (kv_len, 2 * tq), 0)
        col = lax.broadcasted_iota(jnp.int32, (kv_len, 2 * tq), 1)
        query = r0 + jnp.where(col < tq, col, col - tq)
        s = lax.dot_general(k_s[0:kv_len, :], qz, (((1,), (1,)), ((), ())),
                            preferred_element_type=F32)
        s = jnp.where(key <= query, s, neg)
        e = jnp.exp2(s - _reduce_keys(s, jnp.max))
        inv = 1.0 / _reduce_keys(e, jnp.sum)
        ov = jnp.dot(vt_s[:, 0:kv_len], e.astype(BF16),
                     preferred_element_type=F32) * inv
        o = ov[:, :tq] - lam * ov[:, tq:]
        o = o * lax.rsqrt(jnp.mean(o * o, axis=0, keepdims=True) + NORM_EPS)
        o_ref[r0:r0 + tq, :] = (o.T * gain).astype(o_ref.dtype)


def _split_blocks(shape, n_steps):
    rows, cols = shape
    for rb in range(n_steps, 0, -1):
        cb = n_steps // rb
        if (rb * cb == n_steps and rows % (rb * BF16_SUBLANES) == 0
                and cols % (cb * LANES) == 0):
            return rb, cb
    raise ValueError(f"cannot split {shape} into {n_steps} blocks")


def _attention(proj, tables, lams, subln_g, cast_weights, batch, seq, n_heads, tq=512):
    dv = ATTN_V_DIM
    cos, sa, sb = tables
    full = lambda b, h: (0, 0)
    vec = pl.BlockSpec((1, ATTN_HEAD_DIM), full)
    cast_specs = []
    for w in cast_weights:
        rb, cb = _split_blocks(w.shape, batch * n_heads)
        cast_specs.append(pl.BlockSpec(
            (w.shape[0] // rb, w.shape[1] // cb),
            lambda b, h, cb=cb: ((b * n_heads + h) // cb, (b * n_heads + h) % cb)))
    n_cast = len(cast_weights)
    outs = pl.pallas_call(
        functools.partial(_attn_kernel, tq=tq, n_cast=n_cast),
        grid=(batch, n_heads),
        in_specs=[pl.BlockSpec((seq, dv), lambda b, h: (b, h)),
                  pl.BlockSpec((seq, dv), lambda b, h: (b, n_heads + h)),
                  pl.BlockSpec((seq, dv), lambda b, h: (b, 2 * n_heads + h)),
                  pl.BlockSpec((seq, dv), full), pl.BlockSpec((seq, dv), full),
                  pl.BlockSpec((seq, dv), full),
                  vec, vec, vec, vec,
                  pl.BlockSpec((1, dv), full)] + cast_specs,
        out_specs=[pl.BlockSpec((seq, dv), lambda b, h: (b, h))] + cast_specs,
        out_shape=[jax.ShapeDtypeStruct((batch * seq, n_heads * dv), BF16)]
        + [jax.ShapeDtypeStruct(w.shape, BF16) for w in cast_weights],
        scratch_shapes=[pltpu.VMEM((seq, dv), BF16)] * 3 + [pltpu.VMEM((dv, seq), BF16)],
        compiler_params=_params("arbitrary", "arbitrary"),
        name="diff_attn",
    )(proj, proj, proj, cos, sa, sb, *lams, subln_g, *cast_weights)
    return outs[0], outs[1:]


def _conv_kernel(bg_ref, cg_ref, xg_ref, w_ref, o_ref):
    u = cg_ref[...].astype(F32) * xg_ref[...].astype(F32)
    w = w_ref[...]
    row = lax.broadcasted_iota(jnp.int32, u.shape, 0)
    y = w[CONV_K - 1:CONV_K, :] * u
    for shift in range(1, CONV_K):
        shifted = jnp.where(row >= shift, pltpu.roll(u, shift, 0), 0.0)
        y = y + w[CONV_K - 1 - shift:CONV_K - shift, :] * shifted
    o_ref[...] = (bg_ref[...].astype(F32) * y).astype(o_ref.dtype)


def _conv(proj, conv_w, batch, seq, attn_width, conv_width, tc=512):
    qkv_blocks = 3 * attn_width // tc
    cblocks = conv_width // tc
    return pl.pallas_call(
        _conv_kernel,
        grid=(batch, cblocks),
        in_specs=[pl.BlockSpec((seq, tc), lambda b, j: (b, qkv_blocks + j)),
                  pl.BlockSpec((seq, tc), lambda b, j: (b, qkv_blocks + cblocks + j)),
                  pl.BlockSpec((seq, tc), lambda b, j: (b, qkv_blocks + 2 * cblocks + j)),
                  pl.BlockSpec((CONV_K, tc), lambda b, j: (0, j))],
        out_specs=pl.BlockSpec((seq, tc), lambda b, j: (b, j)),
        out_shape=jax.ShapeDtypeStruct((batch * seq, conv_width), BF16),
        compiler_params=_params("arbitrary", "arbitrary"),
        name="gated_conv",
    )(proj, proj, proj, conv_w)


def _out_proj_kernel(attn_ref, conv_ref, wa_ref, wc_ref, x_ref, g_ref, o_ref):
    acc = jnp.dot(attn_ref[...], wa_ref[...], preferred_element_type=F32)
    acc += jnp.dot(conv_ref[...], wc_ref[...], preferred_element_type=F32)
    o_ref[...] = x_ref[...] + g_ref[...] * acc


def _out_proj(attn, conv, w_out, x2d, mod4, gate_idx, seq, tm=1024, tn=1024):
    m, ka = attn.shape
    kc = conv.shape[1]
    assert ka % kc == 0
    n = w_out.shape[1]
    per_batch = seq // tm
    return pl.pallas_call(
        _out_proj_kernel,
        grid=(m // tm, n // tn),
        in_specs=[pl.BlockSpec((tm, ka), lambda i, j: (i, 0)),
                  pl.BlockSpec((tm, kc), lambda i, j: (i, 0)),
                  pl.BlockSpec((ka, tn), lambda i, j: (0, j)),
                  pl.BlockSpec((kc, tn), lambda i, j: (ka // kc, j)),
                  pl.BlockSpec((tm, tn), lambda i, j: (i, j)),
                  pl.BlockSpec((None, None, 1, tn),
                               lambda i, j: (i // per_batch, gate_idx, 0, j))],
        out_specs=pl.BlockSpec((tm, tn), lambda i, j: (i, j)),
        out_shape=jax.ShapeDtypeStruct((m, n), F32),
        compiler_params=_params("arbitrary", "arbitrary"),
        name="out_proj",
    )(attn, conv, w_out, w_out, x2d, mod4)


def _rms(x):
    return x * lax.rsqrt(jnp.mean(x * x, axis=-1, keepdims=True) + NORM_EPS)


def _ffn_kernel(x_hbm, gain_ref, sc_ref, sh_ref, gate_ref, fgain_ref,
                wg_ref, wu_ref, wd_ref, o_hbm, h_s, acc_s, x_buf, o_buf, x_sem, o_sem,
                *, tm, chunk):
    i, f = pl.program_id(0), pl.program_id(1)
    n_chunks = tm // chunk

    def x_copy(c):
        rows = pl.ds(i * tm + c * chunk, chunk)
        return pltpu.make_async_copy(x_hbm.at[rows], x_buf.at[c % 2], x_sem.at[c % 2])

    def o_copy(c):
        rows = pl.ds(i * tm + c * chunk, chunk)
        return pltpu.make_async_copy(o_buf.at[c % 2], o_hbm.at[rows], o_sem.at[c % 2])

    @pl.when(f == 0)
    def _():
        x_copy(0).start()
        for c in range(n_chunks):
            if c + 1 < n_chunks:
                x_copy(c + 1).start()
            x_copy(c).wait()
            y = _rms(x_buf[c % 2]) * gain_ref[...]
            h_s[c * chunk:(c + 1) * chunk, :] = (
                y * (1.0 + sc_ref[...]) + sh_ref[...]).astype(h_s.dtype)
        acc_s[...] = jnp.zeros_like(acc_s)

    h = h_s[...]
    g = jnp.dot(h, wg_ref[...], preferred_element_type=F32)
    u = jnp.dot(h, wu_ref[...], preferred_element_type=F32)
    a = (_silu(g) * u).astype(BF16)
    acc_s[...] += jnp.dot(a, wd_ref[...], preferred_element_type=F32)

    @pl.when(f == pl.num_programs(1) - 1)
    def _():
        x_copy(0).start()
        for c in range(n_chunks):
            if c + 1 < n_chunks:
                x_copy(c + 1).start()
            x_copy(c).wait()
            if c >= 2:
                o_copy(c - 2).wait()
            x2 = x_buf[c % 2] + gate_ref[...] * acc_s[c * chunk:(c + 1) * chunk, :]
            o_buf[c % 2] = _rms(x2) * fgain_ref[...]
            o_copy(c).start()
        for c in range(max(n_chunks - 2, 0), n_chunks):
            o_copy(c).wait()


def _ffn(x1, gain, mod4, sc_idx, sh_idx, gate_idx, final_g, w_gate, w_up, w_down, seq,
         tm=1024, tf=256, chunk=128):
    m, d = x1.shape
    d_ff = w_gate.shape[1]
    per_batch = seq // tm
    row = pl.BlockSpec((1, d), lambda i, f: (0, 0))

    def mod_spec(idx):
        return pl.BlockSpec((None, None, 1, d), lambda i, f: (i // per_batch, idx, 0, 0))

    return pl.pallas_call(
        functools.partial(_ffn_kernel, tm=tm, chunk=chunk),
        grid=(m // tm, d_ff // tf),
        in_specs=[pl.BlockSpec(memory_space=pl.ANY),
                  row, mod_spec(sc_idx), mod_spec(sh_idx), mod_spec(gate_idx), row,
                  pl.BlockSpec((d, tf), lambda i, f: (0, f)),
                  pl.BlockSpec((d, tf), lambda i, f: (0, f)),
                  pl.BlockSpec((tf, d), lambda i, f: (f, 0))],
        out_specs=pl.BlockSpec(memory_space=pl.ANY),
        out_shape=jax.ShapeDtypeStruct((m, d), F32),
        scratch_shapes=[pltpu.VMEM((tm, d), BF16), pltpu.VMEM((tm, d), F32),
                        pltpu.VMEM((2, chunk, d), F32), pltpu.VMEM((2, chunk, d), F32),
                        pltpu.SemaphoreType.DMA((2,)), pltpu.SemaphoreType.DMA((2,))],
        compiler_params=_params("arbitrary", "arbitrary"),
        name="swiglu",
    )(x1, gain, mod4, mod4, mod4, final_g, w_gate, w_up, w_down)


def _rope_tables(seq):
    half = ATTN_HEAD_DIM // 2
    inv_freq = ROPE_THETA ** (-jnp.arange(half, dtype=F32) / half)
    ang = jnp.arange(seq, dtype=F32)[:, None] * inv_freq[None, :]
    cos, sin, zero = jnp.cos(ang), jnp.sin(ang), jnp.zeros_like(ang)
    reps = ATTN_V_DIM // ATTN_HEAD_DIM
    cos_t = jnp.tile(cos, (1, 2 * reps))
    sin_lo = jnp.tile(jnp.concatenate([-sin, zero], axis=1), (1, reps))
    sin_hi = jnp.tile(jnp.concatenate([zero, sin], axis=1), (1, reps))
    return cos_t, sin_lo, sin_hi


def kernel(x, c, w_ada, b_ada, norm1_g, w_in, lambda_q1, lambda_k1, lambda_q2, lambda_k2,
           subln_g, conv_w, w_out, norm2_g, w_gate, w_up, w_down, final_g):
    batch, seq, d = x.shape
    depth = w_ada.shape[0]
    assert depth == 1
    conv_width = conv_w.shape[-1]
    mix_width = w_out.shape[1]
    attn_width = mix_width - conv_width
    n_heads = attn_width // ATTN_V_DIM
    x2d = x.reshape(batch * seq, d)
    tables = _rope_tables(seq)

    for l in range(depth):
        c_pad = jnp.pad(c, ((0, 8 - batch), (0, 0)))
        mod = _ada(c_pad, w_ada[l], b_ada[l][None, :])
        mod4 = mod[:batch].reshape(batch, N_MOD, 1, d)

        h = _norm_mod(x2d, norm1_g[l][None, :], mod4, 1, 0, seq)
        proj = _matmul(h, w_in[l])
        lams = [v[l][None, :] for v in (lambda_q1, lambda_k1, lambda_q2, lambda_k2)]
        attn, (w_out_b, w_gate_b, w_up_b, w_down_b) = _attention(
            proj, tables, lams, subln_g[l][None, :],
            (w_out[l], w_gate[l], w_up[l], w_down[l]), batch, seq, n_heads)
        conv = _conv(proj, conv_w[l], batch, seq, attn_width, conv_width)
        x2d = _out_proj(attn, conv, w_out_b, x2d, mod4, 2, seq)

        x2d = _ffn(x2d, norm2_g[l][None, :], mod4, 4, 3, 5, final_g[None, :],
                   w_gate_b, w_up_b, w_down_b, seq)

    return x2d.reshape(batch, seq, d)
```

```python
import functools
import math

import jax
import jax.numpy as jnp
from jax import lax
from jax.experimental import pallas as pl
from jax.experimental.pallas import tpu as pltpu

F32 = jnp.float32
BF16 = jnp.bfloat16

ATTN_HEAD_DIM = 64
ATTN_V_DIM = 2 * ATTN_HEAD_DIM
CONV_K = 3
ROPE_THETA = 10000.0
NORM_EPS = 1e-6
N_MOD = 6
LAMBDA_INIT = 0.8 - 0.6 * math.exp(-0.3 * 0)
LOG2_E = math.log2(math.e)

VMEM_LIMIT_BYTES = 60 * 1024 * 1024
LANES = 128
F32_SUBLANES = 8
BF16_SUBLANES = 16


def _params(*semantics):
    return pltpu.CompilerParams(dimension_semantics=semantics,
                                vmem_limit_bytes=VMEM_LIMIT_BYTES)


def _silu(v):
    return v * jax.nn.sigmoid(v)


def _ada_tile(c_ref, w_ref, b_ref):
    c_act = _silu(c_ref[...]).astype(BF16)
    return jnp.dot(c_act, w_ref[...].astype(BF16), preferred_element_type=F32) + b_ref[...]


def _ada_kernel(c_ref, w_ref, b_ref, o_ref):
    o_ref[...] = _ada_tile(c_ref, w_ref, b_ref)


def _ada(c_pad, w_ada, b_ada, n_cols, tn=512):
    rows, d = c_pad.shape
    return pl.pallas_call(
        _ada_kernel,
        grid=(n_cols // tn,),
        in_specs=[pl.BlockSpec((rows, d), lambda j: (0, 0)),
                  pl.BlockSpec((d, tn), lambda j: (0, j)),
                  pl.BlockSpec((1, tn), lambda j: (0, j))],
        out_specs=pl.BlockSpec((rows, tn), lambda j: (0, j)),
        out_shape=jax.ShapeDtypeStruct((rows, n_cols), F32),
        compiler_params=_params("arbitrary"),
        name="ada_mod",
    )(c_pad, w_ada, b_ada)


def _rms(x):
    return x * lax.rsqrt(jnp.mean(x * x, axis=-1, keepdims=True) + NORM_EPS)


def _in_proj_kernel(x_ref, gain_ref, sc_ref, sh_ref, w_ref, o_ref, h_even, h_odd, *, chunk):
    i, j = pl.program_id(0), pl.program_id(1)

    def norm_chunk(h_write):
        y = _rms(x_ref[...]) * gain_ref[...]
        h_write[pl.ds(j * chunk, chunk), :] = (
            y * (1.0 + sc_ref[...]) + sh_ref[...]).astype(h_write.dtype)

    def step(h_write, h_read):
        norm_chunk(h_write)
        o_ref[...] = jnp.dot(h_read[...], w_ref[...].astype(BF16),
                             preferred_element_type=F32).astype(o_ref.dtype)

    @pl.when(i == 0)
    def _():
        norm_chunk(h_even)

    @pl.when(jnp.logical_and(i > 0, i % 2 == 0))
    def _():
        step(h_even, h_odd)

    @pl.when(i % 2 == 1)
    def _():
        step(h_odd, h_even)


def _in_proj(x2d, gain, mod4, sc_idx, sh_idx, w, seq, tm=1024, tn=768):
    m, d = x2d.shape
    n = w.shape[1]
    n_blocks, n_tiles = m // tm, n // tn
    chunk = tm // n_tiles
    assert chunk * n_tiles == tm and chunk % BF16_SUBLANES == 0
    per_batch = seq // tm

    def norm_block(i):
        return jnp.minimum(i, n_blocks - 1)

    def mod_spec(idx):
        return pl.BlockSpec((None, None, 1, d),
                            lambda i, j: (norm_block(i) // per_batch, idx, 0, 0))

    return pl.pallas_call(
        functools.partial(_in_proj_kernel, chunk=chunk),
        grid=(n_blocks + 1, n_tiles),
        in_specs=[pl.BlockSpec((chunk, d), lambda i, j: (norm_block(i) * n_tiles + j, 0)),
                  pl.BlockSpec((1, d), lambda i, j: (0, 0)),
                  mod_spec(sc_idx), mod_spec(sh_idx),
                  pl.BlockSpec((d, tn), lambda i, j: (0, j * jnp.minimum(i, 1)))],
        out_specs=pl.BlockSpec((tm, tn),
                               lambda i, j: (jnp.maximum(i - 1, 0), j * jnp.minimum(i, 1))),
        out_shape=jax.ShapeDtypeStruct((m, n), BF16),
        scratch_shapes=[pltpu.VMEM((tm, d), BF16), pltpu.VMEM((tm, d), BF16)],
        compiler_params=_params("arbitrary", "arbitrary"),
        name="in_proj",
    )(x2d, gain, mod4, mod4, w)


def _reduce_keys(t, op, group=64):
    keys, queries = t.shape
    partial = op(t.reshape(keys // group, group, queries), axis=0)
    return op(partial, axis=0, keepdims=True)


def _attn_kernel(q_ref, k_ref, v_ref, cos_ref, sa_ref, sb_ref, lq1_ref, lk1_ref,
                 lq2_ref, lk2_ref, g_ref, c_ref, wada_ref, bada_ref, *rest, tq, n_cast):
    cast_src, o_ref, mod_ref = rest[:n_cast], rest[n_cast], rest[n_cast + 1]
    cast_dst = rest[n_cast + 2:2 * n_cast + 2]
    q1_s, q2_s, k_s, vt_s = rest[2 * n_cast + 2:]
    for src, dst in zip(cast_src, cast_dst):
        dst[...] = src[...].astype(dst.dtype)
    mod_ref[...] = _ada_tile(c_ref, wada_ref, bada_ref)
    seq, width = q_ref.shape
    half = ATTN_HEAD_DIM // 2
    cos, sa, sb = cos_ref[...], sa_ref[...], sb_ref[...]

    def rope(t):
        return (t * cos + pltpu.roll(t, width - half, 1) * sa
                + pltpu.roll(t, half, 1) * sb)

    lane = lax.broadcasted_iota(jnp.int32, (seq, width), 1)
    first = lane < ATTN_HEAD_DIM
    q = rope(q_ref[...].astype(F32)) * (ATTN_HEAD_DIM ** -0.5 * LOG2_E)
    q1_s[...] = jnp.where(first, q, 0.0).astype(BF16)
    q2_s[...] = jnp.where(first, 0.0, q).astype(BF16)
    k_s[...] = rope(k_ref[...].astype(F32)).astype(BF16)
    vt_s[...] = v_ref[...].astype(F32).T.astype(BF16)

    lam = (jnp.exp(jnp.sum(lq1_ref[...] * lk1_ref[...], keepdims=True))
           - jnp.exp(jnp.sum(lq2_ref[...] * lk2_ref[...], keepdims=True))
           + LAMBDA_INIT)
    neg = jnp.finfo(F32).min
    gain = g_ref[...] * (1.0 - LAMBDA_INIT)

    for qi in range(seq // tq):
        r0 = qi * tq
        kv_len = r0 + tq
        qz = jnp.concatenate([q1_s[r0:r0 + tq, :], q2_s[r0:r0 + tq, :]], axis=0)
        key = lax.broadcasted_iota(jnp.int32, (kv_len, 2 * tq), 0)
        col = lax.broadcasted_iota(jnp.int32, (kv_len, 2 * tq), 1)
        query = r0 + jnp.where(col < tq, col, col - tq)
        s = lax.dot_general(k_s[0:kv_len, :], qz, (((1,), (1,)), ((), ())),
                            preferred_element_type=F32)
        s = jnp.where(key <= query, s, neg)
        e = jnp.exp2(s - _reduce_keys(s, jnp.max))
        inv = 1.0 / _reduce_keys(e, jnp.sum)
        ov = jnp.dot(vt_s[:, 0:kv_len], e.astype(BF16),
                     preferred_element_type=F32) * inv
        o = ov[:, :tq] - lam * ov[:, tq:]
        o = o * lax.rsqrt(jnp.mean(o * o, axis=0, keepdims=True) + NORM_EPS)
        o_ref[r0:r0 + tq, :] = (o.T * gain).astype(o_ref.dtype)


def _split_blocks(shape, n_steps):
    rows, cols = shape
    for rb in range(n_steps, 0, -1):
        cb = n_steps // rb
        if (rb * cb == n_steps and rows % (rb * BF16_SUBLANES) == 0
                and cols % (cb * LANES) == 0):
            return rb, cb
    raise ValueError(f"cannot split {shape} into {n_steps} blocks")


def _attention(proj, tables, lams, subln_g, ada, cast_weights, batch, seq, n_heads, tq=512):
    dv = ATTN_V_DIM
    cos, sa, sb = tables
    n_steps = batch * n_heads
    full = lambda b, h: (0, 0)
    vec = pl.BlockSpec((1, ATTN_HEAD_DIM), full)
    cast_specs = []
    for w in cast_weights:
        rb, cb = _split_blocks(w.shape, n_steps)
        cast_specs.append(pl.BlockSpec(
            (w.shape[0] // rb, w.shape[1] // cb),
            lambda b, h, cb=cb: ((b * n_heads + h) // cb, (b * n_heads + h) % cb)))
    n_cast = len(cast_weights)
    c_pad, w_ada, b_ada, col0 = ada
    rows, d = c_pad.shape
    mod_cols = w_ada.shape[1] - col0
    tn = mod_cols // n_steps
    assert tn * n_steps == mod_cols and tn % LANES == 0 and col0 % tn == 0
    ada_tile = lambda b, h: (0, col0 // tn + b * n_heads + h)
    outs = pl.pallas_call(
        functools.partial(_attn_kernel, tq=tq, n_cast=n_cast),
        grid=(batch, n_heads),
        in_specs=[pl.BlockSpec((seq, dv), lambda b, h: (b, h)),
                  pl.BlockSpec((seq, dv), lambda b, h: (b, n_heads + h)),
                  pl.BlockSpec((seq, dv), lambda b, h: (b, 2 * n_heads + h)),
                  pl.BlockSpec((seq, dv), full), pl.BlockSpec((seq, dv), full),
                  pl.BlockSpec((seq, dv), full),
                  vec, vec, vec, vec,
                  pl.BlockSpec((1, dv), full),
                  pl.BlockSpec((rows, d), full), pl.BlockSpec((d, tn), ada_tile),
                  pl.BlockSpec((1, tn), ada_tile)] + cast_specs,
        out_specs=[pl.BlockSpec((seq, dv), lambda b, h: (b, h)),
                   pl.BlockSpec((rows, tn), lambda b, h: (0, b * n_heads + h))] + cast_specs,
        out_shape=[jax.ShapeDtypeStruct((batch * seq, n_heads * dv), BF16),
                   jax.ShapeDtypeStruct((rows, mod_cols), F32)]
        + [jax.ShapeDtypeStruct(w.shape, BF16) for w in cast_weights],
        scratch_shapes=[pltpu.VMEM((seq, dv), BF16)] * 3 + [pltpu.VMEM((dv, seq), BF16)],
        compiler_params=_params("arbitrary", "arbitrary"),
        name="diff_attn",
    )(proj, proj, proj, cos, sa, sb, *lams, subln_g, c_pad, w_ada, b_ada, *cast_weights)
    return outs[0], outs[1], outs[2:]


def _conv_kernel(bg_ref, cg_ref, xg_ref, w_ref, o_ref):
    u = cg_ref[...].astype(F32) * xg_ref[...].astype(F32)
    w = w_ref[...]
    row = lax.broadcasted_iota(jnp.int32, u.shape, 0)
    y = w[CONV_K - 1:CONV_K, :] * u
    for shift in range(1, CONV_K):
        shifted = jnp.where(row >= shift, pltpu.roll(u, shift, 0), 0.0)
        y = y + w[CONV_K - 1 - shift:CONV_K - shift, :] * shifted
    o_ref[...] = (bg_ref[...].astype(F32) * y).astype(o_ref.dtype)


def _conv(proj, conv_w, batch, seq, attn_width, conv_width, tc=512):
    qkv_blocks = 3 * attn_width // tc
    cblocks = conv_width // tc
    return pl.pallas_call(
        _conv_kernel,
        grid=(batch, cblocks),
        in_specs=[pl.BlockSpec((seq, tc), lambda b, j: (b, qkv_blocks + j)),
                  pl.BlockSpec((seq, tc), lambda b, j: (b, qkv_blocks + cblocks + j)),
                  pl.BlockSpec((seq, tc), lambda b, j: (b, qkv_blocks + 2 * cblocks + j)),
                  pl.BlockSpec((CONV_K, tc), lambda b, j: (0, j))],
        out_specs=pl.BlockSpec((seq, tc), lambda b, j: (b, j)),
        out_shape=jax.ShapeDtypeStruct((batch * seq, conv_width), BF16),
        compiler_params=_params("arbitrary", "arbitrary"),
        name="gated_conv",
    )(proj, proj, proj, conv_w)


def _out_proj_kernel(attn_ref, conv_ref, wa_ref, wc_ref, x_ref, g_ref, o_ref):
    acc = jnp.dot(attn_ref[...], wa_ref[...], preferred_element_type=F32)
    acc += jnp.dot(conv_ref[...], wc_ref[...], preferred_element_type=F32)
    o_ref[...] = x_ref[...] + g_ref[...] * acc


def _out_proj(attn, conv, w_out, x2d, mod4, gate_idx, seq, tm=1024, tn=1024):
    m, ka = attn.shape
    kc = conv.shape[1]
    assert ka % kc == 0
    n = w_out.shape[1]
    per_batch = seq // tm
    return pl.pallas_call(
        _out_proj_kernel,
        grid=(m // tm, n // tn),
        in_specs=[pl.BlockSpec((tm, ka), lambda i, j: (i, 0)),
                  pl.BlockSpec((tm, kc), lambda i, j: (i, 0)),
                  pl.BlockSpec((ka, tn), lambda i, j: (0, j)),
                  pl.BlockSpec((kc, tn), lambda i, j: (ka // kc, j)),
                  pl.BlockSpec((tm, tn), lambda i, j: (i, j)),
                  pl.BlockSpec((None, None, 1, tn),
                               lambda i, j: (i // per_batch, gate_idx, 0, j))],
        out_specs=pl.BlockSpec((tm, tn), lambda i, j: (i, j)),
        out_shape=jax.ShapeDtypeStruct((m, n), F32),
        compiler_params=_params("arbitrary", "arbitrary"),
        name="out_proj",
    )(attn, conv, w_out, w_out, x2d, mod4)


def _ffn_kernel(x_hbm, gain_ref, sc_ref, sh_ref, gate_ref, fgain_ref,
                wg_ref, wu_ref, wd_ref, o_hbm, h_s, acc_s, x_buf, o_buf, x_sem, o_sem,
                *, tm, chunk):
    i, f = pl.program_id(0), pl.program_id(1)
    n_chunks = tm // chunk

    def x_copy(c):
        rows = pl.ds(i * tm + c * chunk, chunk)
        return pltpu.make_async_copy(x_hbm.at[rows], x_buf.at[c % 2], x_sem.at[c % 2])

    def o_copy(c):
        rows = pl.ds(i * tm + c * chunk, chunk)
        return pltpu.make_async_copy(o_buf.at[c % 2], o_hbm.at[rows], o_sem.at[c % 2])

    @pl.when(f == 0)
    def _():
        x_copy(0).start()
        for c in range(n_chunks):
            if c + 1 < n_chunks:
                x_copy(c + 1).start()
            x_copy(c).wait()
            y = _rms(x_buf[c % 2]) * gain_ref[...]
            h_s[c * chunk:(c + 1) * chunk, :] = (
                y * (1.0 + sc_ref[...]) + sh_ref[...]).astype(h_s.dtype)
        acc_s[...] = jnp.zeros_like(acc_s)

    h = h_s[...]
    g = jnp.dot(h, wg_ref[...], preferred_element_type=F32)
    u = jnp.dot(h, wu_ref[...], preferred_element_type=F32)
    a = (_silu(g) * u).astype(BF16)
    acc_s[...] += jnp.dot(a, wd_ref[...], preferred_element_type=F32)

    @pl.when(f == pl.num_programs(1) - 1)
    def _():
        x_copy(0).start()
        for c in range(n_chunks):
            if c + 1 < n_chunks:
                x_copy(c + 1).start()
            x_copy(c).wait()
            if c >= 2:
                o_copy(c - 2).wait()
            x2 = x_buf[c % 2] + gate_ref[...] * acc_s[c * chunk:(c + 1) * chunk, :]
            o_buf[c % 2] = _rms(x2) * fgain_ref[...]
            o_copy(c).start()
        for c in range(max(n_chunks - 2, 0), n_chunks):
            o_copy(c).wait()


def _ffn(x1, gain, mod4, sc_idx, sh_idx, gate_idx, final_g, w_gate, w_up, w_down, seq,
         tm=1024, tf=256, chunk=128):
    m, d = x1.shape
    d_ff = w_gate.shape[1]
    per_batch = seq // tm
    row = pl.BlockSpec((1, d), lambda i, f: (0, 0))

    def mod_spec(idx):
        return pl.BlockSpec((None, None, 1, d), lambda i, f: (i // per_batch, idx, 0, 0))

    return pl.pallas_call(
        functools.partial(_ffn_kernel, tm=tm, chunk=chunk),
        grid=(m // tm, d_ff // tf),
        in_specs=[pl.BlockSpec(memory_space=pl.ANY),
                  row, mod_spec(sc_idx), mod_spec(sh_idx), mod_spec(gate_idx), row,
                  pl.BlockSpec((d, tf), lambda i, f: (0, f)),
                  pl.BlockSpec((d, tf), lambda i, f: (0, f)),
                  pl.BlockSpec((tf, d), lambda i, f: (f, 0))],
        out_specs=pl.BlockSpec(memory_space=pl.ANY),
        out_shape=jax.ShapeDtypeStruct((m, d), F32),
        scratch_shapes=[pltpu.VMEM((tm, d), BF16), pltpu.VMEM((tm, d), F32),
                        pltpu.VMEM((2, chunk, d), F32), pltpu.VMEM((2, chunk, d), F32),
                        pltpu.SemaphoreType.DMA((2,)), pltpu.SemaphoreType.DMA((2,))],
        compiler_params=_params("arbitrary", "arbitrary"),
        name="swiglu",
    )(x1, gain, mod4, mod4, mod4, final_g, w_gate, w_up, w_down)


def _rope_tables(seq):
    half = ATTN_HEAD_DIM // 2
    inv_freq = ROPE_THETA ** (-jnp.arange(half, dtype=F32) / half)
    ang = jnp.arange(seq, dtype=F32)[:, None] * inv_freq[None, :]
    cos, sin, zero = jnp.cos(ang), jnp.sin(ang), jnp.zeros_like(ang)
    reps = ATTN_V_DIM // ATTN_HEAD_DIM
    cos_t = jnp.tile(cos, (1, 2 * reps))
    sin_lo = jnp.tile(jnp.concatenate([-sin, zero], axis=1), (1, reps))
    sin_hi = jnp.tile(jnp.concatenate([zero, sin], axis=1), (1, reps))
    return cos_t, sin_lo, sin_hi


def kernel(x, c, w_ada, b_ada, norm1_g, w_in, lambda_q1, lambda_k1, lambda_q2, lambda_k2,
           subln_g, conv_w, w_out, norm2_g, w_gate, w_up, w_down, final_g):
    batch, seq, d = x.shape
    depth = w_ada.shape[0]
    assert depth == 1
    conv_width = conv_w.shape[-1]
    mix_width = w_out.shape[1]
    attn_width = mix_width - conv_width
    n_heads = attn_width // ATTN_V_DIM
    x2d = x.reshape(batch * seq, d)
    tables = _rope_tables(seq)

    for l in range(depth):
        c_pad = jnp.pad(c, ((0, F32_SUBLANES - batch), (0, 0)))
        b_row = b_ada[l][None, :]
        n_early = 2
        mod_a = _ada(c_pad, w_ada[l], b_row, n_early * d)
        mod_a = mod_a[:batch].reshape(batch, n_early, 1, d)

        proj = _in_proj(x2d, norm1_g[l][None, :], mod_a, 1, 0, w_in[l], seq)
        lams = [v[l][None, :] for v in (lambda_q1, lambda_k1, lambda_q2, lambda_k2)]
        attn, mod_b, (w_out_b, w_gate_b, w_up_b, w_down_b) = _attention(
            proj, tables, lams, subln_g[l][None, :], (c_pad, w_ada[l], b_row, n_early * d),
            (w_out[l], w_gate[l], w_up[l], w_down[l]), batch, seq, n_heads)
        mod_b = mod_b[:batch].reshape(batch, N_MOD - n_early, 1, d)
        conv = _conv(proj, conv_w[l], batch, seq, attn_width, conv_width)
        x2d = _out_proj(attn, conv, w_out_b, x2d, mod_b, 0, seq)

        x2d = _ffn(x2d, norm2_g[l][None, :], mod_b, 2, 1, 3, final_g[None, :],
                   w_gate_b, w_up_b, w_down_b, seq)

    return x2d.reshape(batch, seq, d)
```

```python
import functools
import math

import jax
import jax.numpy as jnp
from jax import lax
from jax.experimental import pallas as pl
from jax.experimental.pallas import tpu as pltpu

F32 = jnp.float32
BF16 = jnp.bfloat16

ATTN_HEAD_DIM = 64
ATTN_V_DIM = 2 * ATTN_HEAD_DIM
CONV_K = 3
ROPE_THETA = 10000.0
NORM_EPS = 1e-6
N_MOD = 6
LAMBDA_INIT = 0.8 - 0.6 * math.exp(-0.3 * 0)
LOG2_E = math.log2(math.e)

VMEM_LIMIT_BYTES = 60 * 1024 * 1024
LANES = 128
F32_SUBLANES = 8
BF16_SUBLANES = 16


def _params(*semantics):
    return pltpu.CompilerParams(dimension_semantics=semantics,
                                vmem_limit_bytes=VMEM_LIMIT_BYTES)


def _silu(v):
    return v * jax.nn.sigmoid(v)


def _ada_tile(c_ref, w_ref, b_ref):
    c_act = _silu(c_ref[...]).astype(BF16)
    return jnp.dot(c_act, w_ref[...].astype(BF16), preferred_element_type=F32) + b_ref[...]


def _ada_kernel(c_ref, w_ref, b_ref, o_ref):
    o_ref[...] = _ada_tile(c_ref, w_ref, b_ref)


def _ada(c_pad, w_ada, b_ada, n_cols, tn=512):
    rows, d = c_pad.shape
    return pl.pallas_call(
        _ada_kernel,
        grid=(n_cols // tn,),
        in_specs=[pl.BlockSpec((rows, d), lambda j: (0, 0)),
                  pl.BlockSpec((d, tn), lambda j: (0, j)),
                  pl.BlockSpec((1, tn), lambda j: (0, j))],
        out_specs=pl.BlockSpec((rows, tn), lambda j: (0, j)),
        out_shape=jax.ShapeDtypeStruct((rows, n_cols), F32),
        compiler_params=_params("arbitrary"),
        name="ada_mod",
    )(c_pad, w_ada, b_ada)


def _rms(x):
    return x * lax.rsqrt(jnp.mean(x * x, axis=-1, keepdims=True) + NORM_EPS)


def _for_row_groups(n_rows, group, body):
    def step(r, carry):
        body(pl.ds(pl.multiple_of(r * group, group), group))
        return carry
    lax.fori_loop(0, n_rows // group, step, 0, unroll=True)


def _norm_mod_rows(x_ref, o_ref, o_row0, n_rows, gain, scale, shift):
    mult = 1.0 + scale

    def body(rows):
        y = _rms(x_ref[rows, :]) * gain
        o_ref[pl.ds(o_row0 + rows.start, rows.size), :] = (y * mult + shift).astype(o_ref.dtype)

    _for_row_groups(n_rows, BF16_SUBLANES, body)


def _in_proj_kernel(x_ref, gain_ref, sc_ref, sh_ref, w_ref, o_ref, h_even, h_odd, *, chunk):
    i, j = pl.program_id(0), pl.program_id(1)

    def norm_chunk(h_write):
        _norm_mod_rows(x_ref, h_write, j * chunk, chunk,
                       gain_ref[...], sc_ref[...], sh_ref[...])

    def step(h_write, h_read):
        norm_chunk(h_write)
        o_ref[...] = jnp.dot(h_read[...], w_ref[...].astype(BF16),
                             preferred_element_type=F32).astype(o_ref.dtype)

    @pl.when(i == 0)
    def _():
        norm_chunk(h_even)

    @pl.when(jnp.logical_and(i > 0, i % 2 == 0))
    def _():
        step(h_even, h_odd)

    @pl.when(i % 2 == 1)
    def _():
        step(h_odd, h_even)


def _in_proj(x2d, gain, mod4, sc_idx, sh_idx, w, seq, tm=1024, tn=768):
    m, d = x2d.shape
    n = w.shape[1]
    n_blocks, n_tiles = m // tm, n // tn
    chunk = tm // n_tiles
    assert chunk * n_tiles == tm and chunk % BF16_SUBLANES == 0
    per_batch = seq // tm

    def norm_block(i):
        return jnp.minimum(i, n_blocks - 1)

    def mod_spec(idx):
        return pl.BlockSpec((None, None, 1, d),
                            lambda i, j: (norm_block(i) // per_batch, idx, 0, 0))

    return pl.pallas_call(
        functools.partial(_in_proj_kernel, chunk=chunk),
        grid=(n_blocks + 1, n_tiles),
        in_specs=[pl.BlockSpec((chunk, d), lambda i, j: (norm_block(i) * n_tiles + j, 0)),
                  pl.BlockSpec((1, d), lambda i, j: (0, 0)),
                  mod_spec(sc_idx), mod_spec(sh_idx),
                  pl.BlockSpec((d, tn), lambda i, j: (0, j * jnp.minimum(i, 1)))],
        out_specs=pl.BlockSpec((tm, tn),
                               lambda i, j: (jnp.maximum(i - 1, 0), j * jnp.minimum(i, 1))),
        out_shape=jax.ShapeDtypeStruct((m, n), BF16),
        scratch_shapes=[pltpu.VMEM((tm, d), BF16), pltpu.VMEM((tm, d), BF16)],
        compiler_params=_params("arbitrary", "arbitrary"),
        name="in_proj",
    )(x2d, gain, mod4, mod4, w)


def _reduce_keys(t, op, group=64):
    keys, queries = t.shape
    partial = op(t.reshape(keys // group, group, queries), axis=0)
    return op(partial, axis=0, keepdims=True)


def _attn_kernel(q_ref, k_ref, v_ref, cos_ref, sa_ref, sb_ref, lq1_ref, lk1_ref,
                 lq2_ref, lk2_ref, g_ref, c_ref, wada_ref, bada_ref, *rest, tq, n_cast):
    cast_src, o_ref, mod_ref = rest[:n_cast], rest[n_cast], rest[n_cast + 1]
    cast_dst = rest[n_cast + 2:2 * n_cast + 2]
    q1_s, q2_s, k_s, vt_s = rest[2 * n_cast + 2:]
    for src, dst in zip(cast_src, cast_dst):
        dst[...] = src[...].astype(dst.dtype)
    mod_ref[...] = _ada_tile(c_ref, wada_ref, bada_ref)
    seq, width = q_ref.shape
    half = ATTN_HEAD_DIM // 2
    cos, sa, sb = cos_ref[...], sa_ref[...], sb_ref[...]

    def rope(t):
        return (t * cos + pltpu.roll(t, width - half, 1) * sa
                + pltpu.roll(t, half, 1) * sb)

    lane = lax.broadcasted_iota(jnp.int32, (seq, width), 1)
    first = lane < ATTN_HEAD_DIM
    q = rope(q_ref[...].astype(F32)) * (ATTN_HEAD_DIM ** -0.5 * LOG2_E)
    q1_s[...] = jnp.where(first, q, 0.0).astype(BF16)
    q2_s[...] = jnp.where(first, 0.0, q).astype(BF16)
    k_s[...] = rope(k_ref[...].astype(F32)).astype(BF16)
    vt_s[...] = v_ref[...].astype(F32).T.astype(BF16)

    lam = (jnp.exp(jnp.sum(lq1_ref[...] * lk1_ref[...], keepdims=True))
           - jnp.exp(jnp.sum(lq2_ref[...] * lk2_ref[...], keepdims=True))
           + LAMBDA_INIT)
    neg = jnp.finfo(F32).min
    gain = g_ref[...] * (1.0 - LAMBDA_INIT)

    for qi in range(seq // tq):
        r0 = qi * tq
        kv_len = r0 + tq
        qz = jnp.concatenate([q1_s[r0:r0 + tq, :], q2_s[r0:r0 + tq, :]], axis=0)
        key = lax.broadcasted_iota(jnp.int32, (kv_len, 2 * tq), 0)
        col = lax.broadcasted_iota(jnp.int32, (kv_len, 2 * tq), 1)
        query = r0 + jnp.where(col < tq, col, col - tq)
        s = lax.dot_general(k_s[0:kv_len, :], qz, (((1,), (1,)), ((), ())),
                            preferred_element_type=F32)
        s = jnp.where(key <= query, s, neg)
        e = jnp.exp2(s - _reduce_keys(s, jnp.max))
        inv = 1.0 / _reduce_keys(e, jnp.sum)
        ov = jnp.dot(vt_s[:, 0:kv_len], e.astype(BF16),
                     preferred_element_type=F32) * inv
        o = ov[:, :tq] - lam * ov[:, tq:]
        o = o * lax.rsqrt(jnp.mean(o * o, axis=0, keepdims=True) + NORM_EPS)
        o_ref[r0:r0 + tq, :] = (o.T * gain).astype(o_ref.dtype)


def _split_blocks(shape, n_steps):
    rows, cols = shape
    for rb in range(n_steps, 0, -1):
        cb = n_steps // rb
        if (rb * cb == n_steps and rows % (rb * BF16_SUBLANES) == 0
                and cols % (cb * LANES) == 0):
            return rb, cb
    raise ValueError(f"cannot split {shape} into {n_steps} blocks")


def _attention(proj, tables, lams, subln_g, ada, cast_weights, batch, seq, n_heads, tq=512):
    dv = ATTN_V_DIM
    cos, sa, sb = tables
    n_steps = batch * n_heads
    full = lambda b, h: (0, 0)
    vec = pl.BlockSpec((1, ATTN_HEAD_DIM), full)
    cast_specs = []
    for w in cast_weights:
        rb, cb = _split_blocks(w.shape, n_steps)
        cast_specs.append(pl.BlockSpec(
            (w.shape[0] // rb, w.shape[1] // cb),
            lambda b, h, cb=cb: ((b * n_heads + h) // cb, (b * n_heads + h) % cb)))
    n_cast = len(cast_weights)
    c_pad, w_ada, b_ada, col0 = ada
    rows, d = c_pad.shape
    mod_cols = w_ada.shape[1] - col0
    tn = mod_cols // n_steps
    assert tn * n_steps == mod_cols and tn % LANES == 0 and col0 % tn == 0
    ada_tile = lambda b, h: (0, col0 // tn + b * n_heads + h)
    outs = pl.pallas_call(
        functools.partial(_attn_kernel, tq=tq, n_cast=n_cast),
        grid=(batch, n_heads),
        in_specs=[pl.BlockSpec((seq, dv), lambda b, h: (b, h)),
                  pl.BlockSpec((seq, dv), lambda b, h: (b, n_heads + h)),
                  pl.BlockSpec((seq, dv), lambda b, h: (b, 2 * n_heads + h)),
                  pl.BlockSpec((seq, dv), full), pl.BlockSpec((seq, dv), full),
                  pl.BlockSpec((seq, dv), full),
                  vec, vec, vec, vec,
                  pl.BlockSpec((1, dv), full),
                  pl.BlockSpec((rows, d), full), pl.BlockSpec((d, tn), ada_tile),
                  pl.BlockSpec((1, tn), ada_tile)] + cast_specs,
        out_specs=[pl.BlockSpec((seq, dv), lambda b, h: (b, h)),
                   pl.BlockSpec((rows, tn), lambda b, h: (0, b * n_heads + h))] + cast_specs,
        out_shape=[jax.ShapeDtypeStruct((batch * seq, n_heads * dv), BF16),
                   jax.ShapeDtypeStruct((rows, mod_cols), F32)]
        + [jax.ShapeDtypeStruct(w.shape, BF16) for w in cast_weights],
        scratch_shapes=[pltpu.VMEM((seq, dv), BF16)] * 3 + [pltpu.VMEM((dv, seq), BF16)],
        compiler_params=_params("arbitrary", "arbitrary"),
        name="diff_attn",
    )(proj, proj, proj, cos, sa, sb, *lams, subln_g, c_pad, w_ada, b_ada, *cast_weights)
    return outs[0], outs[1], outs[2:]


def _conv_kernel(bg_ref, cg_ref, xg_ref, w_ref, o_ref):
    u = cg_ref[...].astype(F32) * xg_ref[...].astype(F32)
    w = w_ref[...]
    row = lax.broadcasted_iota(jnp.int32, u.shape, 0)
    y = w[CONV_K - 1:CONV_K, :] * u
    for shift in range(1, CONV_K):
        shifted = jnp.where(row >= shift, pltpu.roll(u, shift, 0), 0.0)
        y = y + w[CONV_K - 1 - shift:CONV_K - shift, :] * shifted
    o_ref[...] = (bg_ref[...].astype(F32) * y).astype(o_ref.dtype)


def _conv(proj, conv_w, batch, seq, attn_width, conv_width, tc=512):
    qkv_blocks = 3 * attn_width // tc
    cblocks = conv_width // tc
    return pl.pallas_call(
        _conv_kernel,
        grid=(batch, cblocks),
        in_specs=[pl.BlockSpec((seq, tc), lambda b, j: (b, qkv_blocks + j)),
                  pl.BlockSpec((seq, tc), lambda b, j: (b, qkv_blocks + cblocks + j)),
                  pl.BlockSpec((seq, tc), lambda b, j: (b, qkv_blocks + 2 * cblocks + j)),
                  pl.BlockSpec((CONV_K, tc), lambda b, j: (0, j))],
        out_specs=pl.BlockSpec((seq, tc), lambda b, j: (b, j)),
        out_shape=jax.ShapeDtypeStruct((batch * seq, conv_width), BF16),
        compiler_params=_params("arbitrary", "arbitrary"),
        name="gated_conv",
    )(proj, proj, proj, conv_w)


def _out_proj_kernel(attn_ref, conv_ref, wa_ref, wc_ref, x_ref, g_ref, o_ref):
    acc = jnp.dot(attn_ref[...], wa_ref[...], preferred_element_type=F32)
    acc += jnp.dot(conv_ref[...], wc_ref[...], preferred_element_type=F32)
    o_ref[...] = x_ref[...] + g_ref[...] * acc


def _out_proj(attn, conv, w_out, x2d, mod4, gate_idx, seq, tm=1024, tn=1024):
    m, ka = attn.shape
    kc = conv.shape[1]
    assert ka % kc == 0
    n = w_out.shape[1]
    per_batch = seq // tm
    return pl.pallas_call(
        _out_proj_kernel,
        grid=(m // tm, n // tn),
        in_specs=[pl.BlockSpec((tm, ka), lambda i, j: (i, 0)),
                  pl.BlockSpec((tm, kc), lambda i, j: (i, 0)),
                  pl.BlockSpec((ka, tn), lambda i, j: (0, j)),
                  pl.BlockSpec((kc, tn), lambda i, j: (ka // kc, j)),
                  pl.BlockSpec((tm, tn), lambda i, j: (i, j)),
                  pl.BlockSpec((None, None, 1, tn),
                               lambda i, j: (i // per_batch, gate_idx, 0, j))],
        out_specs=pl.BlockSpec((tm, tn), lambda i, j: (i, j)),
        out_shape=jax.ShapeDtypeStruct((m, n), F32),
        compiler_params=_params("arbitrary", "arbitrary"),
        name="out_proj",
    )(attn, conv, w_out, w_out, x2d, mod4)


def _ffn_kernel(x_hbm, xn_ref, gain_ref, sc_ref, sh_ref, scn_ref, shn_ref, gate_ref,
                fgain_ref, wg_ref, wu_ref, wd_ref, o_hbm, h_even, h_odd, acc_s,
                x_buf, o_buf, x_sem, o_sem, *, tm, chunk, ahead):
    i, f = pl.program_id(0), pl.program_id(1)
    n_chunks = tm // chunk
    n_ahead = tm // ahead

    def x_copy(c):
        rows = pl.ds(i * tm + c * chunk, chunk)
        return pltpu.make_async_copy(x_hbm.at[rows], x_buf.at[c % 2], x_sem.at[c % 2])

    def o_copy(c):
        rows = pl.ds(i * tm + c * chunk, chunk)
        return pltpu.make_async_copy(o_buf.at[c % 2], o_hbm.at[rows], o_sem.at[c % 2])

    @pl.when(jnp.logical_and(i == 0, f == 0))
    def _():
        x_copy(0).start()
        for c in range(n_chunks):
            if c + 1 < n_chunks:
                x_copy(c + 1).start()
            x_copy(c).wait()
            _norm_mod_rows(x_buf.at[c % 2], h_even, c * chunk, chunk,
                           gain_ref[...], sc_ref[...], sh_ref[...])

    @pl.when(f == 0)
    def _():
        acc_s[...] = jnp.zeros_like(acc_s)

    def step(h_cur, h_next):
        _norm_mod_rows(xn_ref, h_next, jnp.minimum(f, n_ahead - 1) * ahead, ahead,
                       gain_ref[...], scn_ref[...], shn_ref[...])
        h = h_cur[...]
        g = jnp.dot(h, wg_ref[...], preferred_element_type=F32)
        u = jnp.dot(h, wu_ref[...], preferred_element_type=F32)
        a = (_silu(g) * u).astype(BF16)
        acc_s[...] += jnp.dot(a, wd_ref[...], preferred_element_type=F32)

    @pl.when(i % 2 == 0)
    def _():
        step(h_even, h_odd)

    @pl.when(i % 2 == 1)
    def _():
        step(h_odd, h_even)

    @pl.when(f == pl.num_programs(1) - 1)
    def _():
        x_copy(0).start()
        for c in range(n_chunks):
            if c + 1 < n_chunks:
                x_copy(c + 1).start()
            x_copy(c).wait()
            if c >= 2:
                o_copy(c - 2).wait()
            x_c, o_c = x_buf.at[c % 2], o_buf.at[c % 2]

            def residual_norm(rows, c=c, x_c=x_c, o_c=o_c):
                acc = acc_s[pl.ds(c * chunk + rows.start, rows.size), :]
                o_c[rows, :] = _rms(x_c[rows, :] + gate_ref[...] * acc) * fgain_ref[...]

            _for_row_groups(chunk, F32_SUBLANES, residual_norm)
            o_copy(c).start()
        for c in range(max(n_chunks - 2, 0), n_chunks):
            o_copy(c).wait()


def _ffn(x1, gain, mod4, sc_idx, sh_idx, gate_idx, final_g, w_gate, w_up, w_down, seq,
         tm=1024, tf=256, chunk=128):
    m, d = x1.shape
    d_ff = w_gate.shape[1]
    n_blocks, n_steps = m // tm, d_ff // tf
    per_batch = seq // tm
    n_ahead = 1
    while n_ahead * 2 <= n_steps and tm % (n_ahead * 2 * BF16_SUBLANES) == 0:
        n_ahead *= 2
    ahead = tm // n_ahead
    row = pl.BlockSpec((1, d), lambda i, f: (0, 0))

    def next_block(i):
        return jnp.minimum(i + 1, n_blocks - 1)

    def mod_spec(idx, block=lambda i: i):
        return pl.BlockSpec((None, None, 1, d),
                            lambda i, f: (block(i) // per_batch, idx, 0, 0))

    return pl.pallas_call(
        functools.partial(_ffn_kernel, tm=tm, chunk=chunk, ahead=ahead),
        grid=(n_blocks, n_steps),
        in_specs=[pl.BlockSpec(memory_space=pl.ANY),
                  pl.BlockSpec((ahead, d), lambda i, f: (
                      next_block(i) * n_ahead + jnp.minimum(f, n_ahead - 1), 0)),
                  row, mod_spec(sc_idx), mod_spec(sh_idx),
                  mod_spec(sc_idx, next_block), mod_spec(sh_idx, next_block),
                  mod_spec(gate_idx), row,
                  pl.BlockSpec((d, tf), lambda i, f: (0, f)),
                  pl.BlockSpec((d, tf), lambda i, f: (0, f)),
                  pl.BlockSpec((tf, d), lambda i, f: (f, 0))],
        out_specs=pl.BlockSpec(memory_space=pl.ANY),
        out_shape=jax.ShapeDtypeStruct((m, d), F32),
        scratch_shapes=[pltpu.VMEM((tm, d), BF16), pltpu.VMEM((tm, d), BF16),
                        pltpu.VMEM((tm, d), F32),
                        pltpu.VMEM((2, chunk, d), F32), pltpu.VMEM((2, chunk, d), F32),
                        pltpu.SemaphoreType.DMA((2,)), pltpu.SemaphoreType.DMA((2,))],
        compiler_params=_params("arbitrary", "arbitrary"),
        name="swiglu",
    )(x1, x1, gain, mod4, mod4, mod4, mod4, mod4, final_g, w_gate, w_up, w_down)


def _rope_tables(seq):
    half = ATTN_HEAD_DIM // 2
    inv_freq = ROPE_THETA ** (-jnp.arange(half, dtype=F32) / half)
    ang = jnp.arange(seq, dtype=F32)[:, None] * inv_freq[None, :]
    cos, sin, zero = jnp.cos(ang), jnp.sin(ang), jnp.zeros_like(ang)
    reps = ATTN_V_DIM // ATTN_HEAD_DIM
    cos_t = jnp.tile(cos, (1, 2 * reps))
    sin_lo = jnp.tile(jnp.concatenate([-sin, zero], axis=1), (1, reps))
    sin_hi = jnp.tile(jnp.concatenate([zero, sin], axis=1), (1, reps))
    return cos_t, sin_lo, sin_hi


def kernel(x, c, w_ada, b_ada, norm1_g, w_in, lambda_q1, lambda_k1, lambda_q2, lambda_k2,
           subln_g, conv_w, w_out, norm2_g, w_gate, w_up, w_down, final_g):
    batch, seq, d = x.shape
    depth = w_ada.shape[0]
    assert depth == 1
    conv_width = conv_w.shape[-1]
    mix_width = w_out.shape[1]
    attn_width = mix_width - conv_width
    n_heads = attn_width // ATTN_V_DIM
    x2d = x.reshape(batch * seq, d)
    tables = _rope_tables(seq)

    for l in range(depth):
        c_pad = jnp.pad(c, ((0, F32_SUBLANES - batch), (0, 0)))
        b_row = b_ada[l][None, :]
        n_early = 2
        mod_a = _ada(c_pad, w_ada[l], b_row, n_early * d)
        mod_a = mod_a[:batch].reshape(batch, n_early, 1, d)

        proj = _in_proj(x2d, norm1_g[l][None, :], mod_a, 1, 0, w_in[l], seq)
        lams = [v[l][None, :] for v in (lambda_q1, lambda_k1, lambda_q2, lambda_k2)]
        attn, mod_b, (w_out_b, w_gate_b, w_up_b, w_down_b) = _attention(
            proj, tables, lams, subln_g[l][None, :], (c_pad, w_ada[l], b_row, n_early * d),
            (w_out[l], w_gate[l], w_up[l], w_down[l]), batch, seq, n_heads)
        mod_b = mod_b[:batch].reshape(batch, N_MOD - n_early, 1, d)
        conv = _conv(proj, conv_w[l], batch, seq, attn_width, conv_width)
        x2d = _out_proj(attn, conv, w_out_b, x2d, mod_b, 0, seq)

        x2d = _ffn(x2d, norm2_g[l][None, :], mod_b, 2, 1, 3, final_g[None, :],
                   w_gate_b, w_up_b, w_down_b, seq)

    return x2d.reshape(batch, seq, d)
```

```python
import functools
import math

import jax
import jax.numpy as jnp
from jax import lax
from jax.experimental import pallas as pl
from jax.experimental.pallas import tpu as pltpu

F32 = jnp.float32
BF16 = jnp.bfloat16

ATTN_HEAD_DIM = 64
ATTN_V_DIM = 2 * ATTN_HEAD_DIM
CONV_K = 3
ROPE_THETA = 10000.0
NORM_EPS = 1e-6
N_MOD = 6
LAMBDA_INIT = 0.8 - 0.6 * math.exp(-0.3 * 0)
LOG2_E = math.log2(math.e)

VMEM_LIMIT_BYTES = 60 * 1024 * 1024
LANES = 128
F32_SUBLANES = 8
BF16_SUBLANES = 16


def _params(*semantics):
    return pltpu.CompilerParams(dimension_semantics=semantics,
                                vmem_limit_bytes=VMEM_LIMIT_BYTES)


def _silu(v):
    return v * jax.nn.sigmoid(v)


def _ada_tile(c_ref, w_ref, b_ref):
    c_act = _silu(c_ref[...]).astype(BF16)
    return jnp.dot(c_act, w_ref[...].astype(BF16), preferred_element_type=F32) + b_ref[...]


def _ada_kernel(c_ref, w_ref, b_ref, o_ref):
    o_ref[...] = _ada_tile(c_ref, w_ref, b_ref)


def _ada(c_pad, w_ada, b_ada, n_cols, tn=512):
    rows, d = c_pad.shape
    return pl.pallas_call(
        _ada_kernel,
        grid=(n_cols // tn,),
        in_specs=[pl.BlockSpec((rows, d), lambda j: (0, 0)),
                  pl.BlockSpec((d, tn), lambda j: (0, j)),
                  pl.BlockSpec((1, tn), lambda j: (0, j))],
        out_specs=pl.BlockSpec((rows, tn), lambda j: (0, j)),
        out_shape=jax.ShapeDtypeStruct((rows, n_cols), F32),
        compiler_params=_params("arbitrary"),
        name="ada_mod",
    )(c_pad, w_ada, b_ada)


def _rms(x):
    return x * lax.rsqrt(jnp.mean(x * x, axis=-1, keepdims=True) + NORM_EPS)


def _for_row_groups(n_rows, group, body):
    def step(r, carry):
        body(pl.ds(pl.multiple_of(r * group, group), group))
        return carry
    lax.fori_loop(0, n_rows // group, step, 0, unroll=True)


def _norm_mod_rows(x_ref, o_ref, o_row0, n_rows, gain, scale, shift):
    mult = 1.0 + scale

    def body(rows):
        y = _rms(x_ref[rows, :]) * gain
        o_ref[pl.ds(o_row0 + rows.start, rows.size), :] = (y * mult + shift).astype(o_ref.dtype)

    _for_row_groups(n_rows, BF16_SUBLANES, body)


def _split_blocks(shape, n_steps):
    rows, cols = shape
    for rb in range(n_steps, 0, -1):
        cb = n_steps // rb
        if (rb * cb == n_steps and rows % (rb * BF16_SUBLANES) == 0
                and cols % (cb * LANES) == 0):
            return rb, cb
    raise ValueError(f"cannot split {shape} into {n_steps} blocks")


def _cast_specs(weights, n_steps, step_of):
    specs = []
    for w in weights:
        rb, cb = _split_blocks(w.shape, n_steps)
        specs.append(pl.BlockSpec(
            (w.shape[0] // rb, w.shape[1] // cb),
            lambda *g, cb=cb: (step_of(*g) // cb, step_of(*g) % cb)))
    return specs


def _in_proj_kernel(x_ref, gain_ref, sc_ref, sh_ref, w_ref, *rest, chunk, n_cast):
    cast_src, o_ref, cast_dst = rest[:n_cast], rest[n_cast], rest[n_cast + 1:2 * n_cast + 1]
    h_even, h_odd = rest[2 * n_cast + 1:]
    i, j = pl.program_id(0), pl.program_id(1)

    def norm_chunk(h_write):
        _norm_mod_rows(x_ref, h_write, j * chunk, chunk,
                       gain_ref[...], sc_ref[...], sh_ref[...])

    def step(h_write, h_read):
        norm_chunk(h_write)
        for src, dst in zip(cast_src, cast_dst):
            dst[...] = src[...].astype(dst.dtype)
        o_ref[...] = jnp.dot(h_read[...], w_ref[...].astype(BF16),
                             preferred_element_type=F32).astype(o_ref.dtype)

    @pl.when(i == 0)
    def _():
        norm_chunk(h_even)

    @pl.when(jnp.logical_and(i > 0, i % 2 == 0))
    def _():
        step(h_even, h_odd)

    @pl.when(i % 2 == 1)
    def _():
        step(h_odd, h_even)


def _in_proj(x2d, gain, mod4, sc_idx, sh_idx, w, cast_weights, seq, tm=1024, tn=768):
    m, d = x2d.shape
    n = w.shape[1]
    n_blocks, n_tiles = m // tm, n // tn
    chunk = tm // n_tiles
    assert chunk * n_tiles == tm and chunk % BF16_SUBLANES == 0
    per_batch = seq // tm

    def norm_block(i):
        return jnp.minimum(i, n_blocks - 1)

    def tile(i, j):
        return j * jnp.minimum(i, 1)

    def mult_step(i, j):
        return jnp.maximum(i - 1, 0) * n_tiles + tile(i, j)

    def mod_spec(idx):
        return pl.BlockSpec((None, None, 1, d),
                            lambda i, j: (norm_block(i) // per_batch, idx, 0, 0))

    cast_specs = _cast_specs(cast_weights, n_blocks * n_tiles, mult_step)
    outs = pl.pallas_call(
        functools.partial(_in_proj_kernel, chunk=chunk, n_cast=len(cast_weights)),
        grid=(n_blocks + 1, n_tiles),
        in_specs=[pl.BlockSpec((chunk, d), lambda i, j: (norm_block(i) * n_tiles + j, 0)),
                  pl.BlockSpec((1, d), lambda i, j: (0, 0)),
                  mod_spec(sc_idx), mod_spec(sh_idx),
                  pl.BlockSpec((d, tn), lambda i, j: (0, tile(i, j)))] + cast_specs,
        out_specs=[pl.BlockSpec((tm, tn), lambda i, j: (jnp.maximum(i - 1, 0), tile(i, j)))]
        + cast_specs,
        out_shape=[jax.ShapeDtypeStruct((m, n), BF16)]
        + [jax.ShapeDtypeStruct(cw.shape, BF16) for cw in cast_weights],
        scratch_shapes=[pltpu.VMEM((tm, d), BF16), pltpu.VMEM((tm, d), BF16)],
        compiler_params=_params("arbitrary", "arbitrary"),
        name="in_proj",
    )(x2d, gain, mod4, mod4, w, *cast_weights)
    return outs[0], outs[1:]


def _reduce_keys(t, op, group=64):
    keys, queries = t.shape
    partial = op(t.reshape(keys // group, group, queries), axis=0)
    return op(partial, axis=0, keepdims=True)


def _attn_kernel(q_ref, k_ref, v_ref, cos_ref, sa_ref, sb_ref, lq1_ref, lk1_ref,
                 lq2_ref, lk2_ref, g_ref, c_ref, wada_ref, bada_ref, *rest, tq, n_cast):
    cast_src, o_ref, mod_ref = rest[:n_cast], rest[n_cast], rest[n_cast + 1]
    cast_dst = rest[n_cast + 2:2 * n_cast + 2]
    q1_s, q2_s, k_s, vt_s = rest[2 * n_cast + 2:]
    for src, dst in zip(cast_src, cast_dst):
        dst[...] = src[...].astype(dst.dtype)
    mod_ref[...] = _ada_tile(c_ref, wada_ref, bada_ref)
    seq, width = q_ref.shape
    half = ATTN_HEAD_DIM // 2
    cos, sa, sb = cos_ref[...], sa_ref[...], sb_ref[...]

    def rope(t):
        return (t * cos + pltpu.roll(t, width - half, 1) * sa
                + pltpu.roll(t, half, 1) * sb)

    lane = lax.broadcasted_iota(jnp.int32, (seq, width), 1)
    first = lane < ATTN_HEAD_DIM
    q = rope(q_ref[...].astype(F32)) * (ATTN_HEAD_DIM ** -0.5 * LOG2_E)
    q1_s[...] = jnp.where(first, q, 0.0).astype(BF16)
    q2_s[...] = jnp.where(first, 0.0, q).astype(BF16)
    k_s[...] = rope(k_ref[...].astype(F32)).astype(BF16)
    vt_s[...] = v_ref[...].astype(F32).T.astype(BF16)

    lam = (jnp.exp(jnp.sum(lq1_ref[...] * lk1_ref[...], keepdims=True))
           - jnp.exp(jnp.sum(lq2_ref[...] * lk2_ref[...], keepdims=True))
           + LAMBDA_INIT)
    neg = jnp.finfo(F32).min
    gain = g_ref[...] * (1.0 - LAMBDA_INIT)

    for qi in range(seq // tq):
        r0 = qi * tq
        kv_len = r0 + tq
        qz = jnp.concatenate([q1_s[r0:r0 + tq, :], q2_s[r0:r0 + tq, :]], axis=0)
        key = lax.broadcasted_iota(jnp.int32, (kv_len, 2 * tq), 0)
        col = lax.broadcasted_iota(jnp.int32, (kv_len, 2 * tq), 1)
        query = r0 + jnp.where(col < tq, col, col - tq)
        s = lax.dot_general(k_s[0:kv_len, :], qz, (((1,), (1,)), ((), ())),
                            preferred_element_type=F32)
        s = jnp.where(key <= query, s, neg)
        e = jnp.exp2(s - _reduce_keys(s, jnp.max))
        inv = 1.0 / _reduce_keys(e, jnp.sum)
        ov = jnp.dot(vt_s[:, 0:kv_len], e.astype(BF16),
                     preferred_element_type=F32) * inv
        o = ov[:, :tq] - lam * ov[:, tq:]
        o = o * lax.rsqrt(jnp.mean(o * o, axis=0, keepdims=True) + NORM_EPS)
        o_ref[r0:r0 + tq, :] = (o.T * gain).astype(o_ref.dtype)


def _attention(proj, tables, lams, subln_g, ada, cast_weights, batch, seq, n_heads, tq=512):
    dv = ATTN_V_DIM
    cos, sa, sb = tables
    n_steps = batch * n_heads
    full = lambda b, h: (0, 0)
    vec = pl.BlockSpec((1, ATTN_HEAD_DIM), full)
    cast_specs = _cast_specs(cast_weights, n_steps, lambda b, h: b * n_heads + h)
    n_cast = len(cast_weights)
    c_pad, w_ada, b_ada, col0 = ada
    rows, d = c_pad.shape
    mod_cols = w_ada.shape[1] - col0
    tn = mod_cols // n_steps
    assert tn * n_steps == mod_cols and tn % LANES == 0 and col0 % tn == 0
    ada_tile = lambda b, h: (0, col0 // tn + b * n_heads + h)
    outs = pl.pallas_call(
        functools.partial(_attn_kernel, tq=tq, n_cast=n_cast),
        grid=(batch, n_heads),
        in_specs=[pl.BlockSpec((seq, dv), lambda b, h: (b, h)),
                  pl.BlockSpec((seq, dv), lambda b, h: (b, n_heads + h)),
                  pl.BlockSpec((seq, dv), lambda b, h: (b, 2 * n_heads + h)),
                  pl.BlockSpec((seq, dv), full), pl.BlockSpec((seq, dv), full),
                  pl.BlockSpec((seq, dv), full),
                  vec, vec, vec, vec,
                  pl.BlockSpec((1, dv), full),
                  pl.BlockSpec((rows, d), full), pl.BlockSpec((d, tn), ada_tile),
                  pl.BlockSpec((1, tn), ada_tile)] + cast_specs,
        out_specs=[pl.BlockSpec((seq, dv), lambda b, h: (b, h)),
                   pl.BlockSpec((rows, tn), lambda b, h: (0, b * n_heads + h))] + cast_specs,
        out_shape=[jax.ShapeDtypeStruct((batch * seq, n_heads * dv), BF16),
                   jax.ShapeDtypeStruct((rows, mod_cols), F32)]
        + [jax.ShapeDtypeStruct(w.shape, BF16) for w in cast_weights],
        scratch_shapes=[pltpu.VMEM((seq, dv), BF16)] * 3 + [pltpu.VMEM((dv, seq), BF16)],
        compiler_params=_params("arbitrary", "arbitrary"),
        name="diff_attn",
    )(proj, proj, proj, cos, sa, sb, *lams, subln_g, c_pad, w_ada, b_ada, *cast_weights)
    return outs[0], outs[1], outs[2:]


def _conv_kernel(bg_ref, cg_ref, xg_ref, w_ref, o_ref):
    u = cg_ref[...].astype(F32) * xg_ref[...].astype(F32)
    w = w_ref[...]
    row = lax.broadcasted_iota(jnp.int32, u.shape, 0)
    y = w[CONV_K - 1:CONV_K, :] * u
    for shift in range(1, CONV_K):
        shifted = jnp.where(row >= shift, pltpu.roll(u, shift, 0), 0.0)
        y = y + w[CONV_K - 1 - shift:CONV_K - shift, :] * shifted
    o_ref[...] = (bg_ref[...].astype(F32) * y).astype(o_ref.dtype)


def _conv(proj, conv_w, batch, seq, attn_width, conv_width, tc=512):
    qkv_blocks = 3 * attn_width // tc
    cblocks = conv_width // tc
    return pl.pallas_call(
        _conv_kernel,
        grid=(batch, cblocks),
        in_specs=[pl.BlockSpec((seq, tc), lambda b, j: (b, qkv_blocks + j)),
                  pl.BlockSpec((seq, tc), lambda b, j: (b, qkv_blocks + cblocks + j)),
                  pl.BlockSpec((seq, tc), lambda b, j: (b, qkv_blocks + 2 * cblocks + j)),
                  pl.BlockSpec((CONV_K, tc), lambda b, j: (0, j))],
        out_specs=pl.BlockSpec((seq, tc), lambda b, j: (b, j)),
        out_shape=jax.ShapeDtypeStruct((batch * seq, conv_width), BF16),
        compiler_params=_params("arbitrary", "arbitrary"),
        name="gated_conv",
    )(proj, proj, proj, conv_w)


def _out_proj_kernel(attn_ref, conv_ref, wa_ref, wc_ref, x_ref, g_ref, o_ref):
    acc = jnp.dot(attn_ref[...], wa_ref[...], preferred_element_type=F32)
    acc += jnp.dot(conv_ref[...], wc_ref[...], preferred_element_type=F32)
    o_ref[...] = x_ref[...] + g_ref[...] * acc


def _out_proj(attn, conv, w_out, x2d, mod4, gate_idx, seq, tm=1024, tn=1024):
    m, ka = attn.shape
    kc = conv.shape[1]
    assert ka % kc == 0
    n = w_out.shape[1]
    per_batch = seq // tm
    return pl.pallas_call(
        _out_proj_kernel,
        grid=(m // tm, n // tn),
        in_specs=[pl.BlockSpec((tm, ka), lambda i, j: (i, 0)),
                  pl.BlockSpec((tm, kc), lambda i, j: (i, 0)),
                  pl.BlockSpec((ka, tn), lambda i, j: (0, j)),
                  pl.BlockSpec((kc, tn), lambda i, j: (ka // kc, j)),
                  pl.BlockSpec((tm, tn), lambda i, j: (i, j)),
                  pl.BlockSpec((None, None, 1, tn),
                               lambda i, j: (i // per_batch, gate_idx, 0, j))],
        out_specs=pl.BlockSpec((tm, tn), lambda i, j: (i, j)),
        out_shape=jax.ShapeDtypeStruct((m, n), F32),
        compiler_params=_params("arbitrary", "arbitrary"),
        name="out_proj",
    )(attn, conv, w_out, w_out, x2d, mod4)


def _ffn_kernel(x_hbm, xn_ref, gain_ref, sc_ref, sh_ref, scn_ref, shn_ref, gate_ref,
                fgain_ref, wg_ref, wu_ref, wd_ref, o_hbm, h_even, h_odd, acc_s,
                x_buf, o_buf, x_sem, o_sem, *, tm, chunk, ahead):
    i, f = pl.program_id(0), pl.program_id(1)
    n_chunks = tm // chunk
    n_ahead = tm // ahead

    def x_copy(c):
        rows = pl.ds(i * tm + c * chunk, chunk)
        return pltpu.make_async_copy(x_hbm.at[rows], x_buf.at[c % 2], x_sem.at[c % 2])

    def o_copy(c):
        rows = pl.ds(i * tm + c * chunk, chunk)
        return pltpu.make_async_copy(o_buf.at[c % 2], o_hbm.at[rows], o_sem.at[c % 2])

    @pl.when(jnp.logical_and(i == 0, f == 0))
    def _():
        x_copy(0).start()
        for c in range(n_chunks):
            if c + 1 < n_chunks:
                x_copy(c + 1).start()
            x_copy(c).wait()
            _norm_mod_rows(x_buf.at[c % 2], h_even, c * chunk, chunk,
                           gain_ref[...], sc_ref[...], sh_ref[...])

    @pl.when(f == 0)
    def _():
        acc_s[...] = jnp.zeros_like(acc_s)

    def step(h_cur, h_next):
        _norm_mod_rows(xn_ref, h_next, jnp.minimum(f, n_ahead - 1) * ahead, ahead,
                       gain_ref[...], scn_ref[...], shn_ref[...])
        h = h_cur[...]
        g = jnp.dot(h, wg_ref[...], preferred_element_type=F32)
        u = jnp.dot(h, wu_ref[...], preferred_element_type=F32)
        a = (_silu(g) * u).astype(BF16)
        acc_s[...] += jnp.dot(a, wd_ref[...], preferred_element_type=F32)

    @pl.when(i % 2 == 0)
    def _():
        step(h_even, h_odd)

    @pl.when(i % 2 == 1)
    def _():
        step(h_odd, h_even)

    @pl.when(f == pl.num_programs(1) - 1)
    def _():
        x_copy(0).start()
        for c in range(n_chunks):
            if c + 1 < n_chunks:
                x_copy(c + 1).start()
            x_copy(c).wait()
            if c >= 2:
                o_copy(c - 2).wait()
            x_c, o_c = x_buf.at[c % 2], o_buf.at[c % 2]

            def residual_norm(rows, c=c, x_c=x_c, o_c=o_c):
                acc = acc_s[pl.ds(c * chunk + rows.start, rows.size), :]
                o_c[rows, :] = _rms(x_c[rows, :] + gate_ref[...] * acc) * fgain_ref[...]

            _for_row_groups(chunk, F32_SUBLANES, residual_norm)
            o_copy(c).start()
        for c in range(max(n_chunks - 2, 0), n_chunks):
            o_copy(c).wait()


def _ffn(x1, gain, mod4, sc_idx, sh_idx, gate_idx, final_g, w_gate, w_up, w_down, seq,
         tm=1024, tf=256, chunk=128):
    m, d = x1.shape
    d_ff = w_gate.shape[1]
    n_blocks, n_steps = m // tm, d_ff // tf
    per_batch = seq // tm
    n_ahead = 1
    while n_ahead * 2 <= n_steps and tm % (n_ahead * 2 * BF16_SUBLANES) == 0:
        n_ahead *= 2
    ahead = tm // n_ahead
    row = pl.BlockSpec((1, d), lambda i, f: (0, 0))

    def next_block(i):
        return jnp.minimum(i + 1, n_blocks - 1)

    def mod_spec(idx, block=lambda i: i):
        return pl.BlockSpec((None, None, 1, d),
                            lambda i, f: (block(i) // per_batch, idx, 0, 0))

    return pl.pallas_call(
        functools.partial(_ffn_kernel, tm=tm, chunk=chunk, ahead=ahead),
        grid=(n_blocks, n_steps),
        in_specs=[pl.BlockSpec(memory_space=pl.ANY),
                  pl.BlockSpec((ahead, d), lambda i, f: (
                      next_block(i) * n_ahead + jnp.minimum(f, n_ahead - 1), 0)),
                  row, mod_spec(sc_idx), mod_spec(sh_idx),
                  mod_spec(sc_idx, next_block), mod_spec(sh_idx, next_block),
                  mod_spec(gate_idx), row,
                  pl.BlockSpec((d, tf), lambda i, f: (0, f)),
                  pl.BlockSpec((d, tf), lambda i, f: (0, f)),
                  pl.BlockSpec((tf, d), lambda i, f: (f, 0))],
        out_specs=pl.BlockSpec(memory_space=pl.ANY),
        out_shape=jax.ShapeDtypeStruct((m, d), F32),
        scratch_shapes=[pltpu.VMEM((tm, d), BF16), pltpu.VMEM((tm, d), BF16),
                        pltpu.VMEM((tm, d), F32),
                        pltpu.VMEM((2, chunk, d), F32), pltpu.VMEM((2, chunk, d), F32),
                        pltpu.SemaphoreType.DMA((2,)), pltpu.SemaphoreType.DMA((2,))],
        compiler_params=_params("arbitrary", "arbitrary"),
        name="swiglu",
    )(x1, x1, gain, mod4, mod4, mod4, mod4, mod4, final_g, w_gate, w_up, w_down)


def _rope_tables(seq):
    half = ATTN_HEAD_DIM // 2
    inv_freq = ROPE_THETA ** (-jnp.arange(half, dtype=F32) / half)
    ang = jnp.arange(seq, dtype=F32)[:, None] * inv_freq[None, :]
    cos, sin, zero = jnp.cos(ang), jnp.sin(ang), jnp.zeros_like(ang)
    reps = ATTN_V_DIM // ATTN_HEAD_DIM
    cos_t = jnp.tile(cos, (1, 2 * reps))
    sin_lo = jnp.tile(jnp.concatenate([-sin, zero], axis=1), (1, reps))
    sin_hi = jnp.tile(jnp.concatenate([zero, sin], axis=1), (1, reps))
    return cos_t, sin_lo, sin_hi


def kernel(x, c, w_ada, b_ada, norm1_g, w_in, lambda_q1, lambda_k1, lambda_q2, lambda_k2,
           subln_g, conv_w, w_out, norm2_g, w_gate, w_up, w_down, final_g):
    batch, seq, d = x.shape
    depth = w_ada.shape[0]
    assert depth == 1
    conv_width = conv_w.shape[-1]
    mix_width = w_out.shape[1]
    attn_width = mix_width - conv_width
    n_heads = attn_width // ATTN_V_DIM
    x2d = x.reshape(batch * seq, d)
    tables = _rope_tables(seq)

    for l in range(depth):
        c_pad = jnp.pad(c, ((0, F32_SUBLANES - batch), (0, 0)))
        b_row = b_ada[l][None, :]
        n_early = 2
        mod_a = _ada(c_pad, w_ada[l], b_row, n_early * d)
        mod_a = mod_a[:batch].reshape(batch, n_early, 1, d)

        proj, (w_out_b, w_down_b) = _in_proj(x2d, norm1_g[l][None, :], mod_a, 1, 0, w_in[l],
                                             (w_out[l], w_down[l]), seq)
        lams = [v[l][None, :] for v in (lambda_q1, lambda_k1, lambda_q2, lambda_k2)]
        attn, mod_b, (w_gate_b, w_up_b) = _attention(
            proj, tables, lams, subln_g[l][None, :], (c_pad, w_ada[l], b_row, n_early * d),
            (w_gate[l], w_up[l]), batch, seq, n_heads)
        mod_b = mod_b[:batch].reshape(batch, N_MOD - n_early, 1, d)
        conv = _conv(proj, conv_w[l], batch, seq, attn_width, conv_width)
        x2d = _out_proj(attn, conv, w_out_b, x2d, mod_b, 0, seq)

        x2d = _ffn(x2d, norm2_g[l][None, :], mod_b, 2, 1, 3, final_g[None, :],
                   w_gate_b, w_up_b, w_down_b, seq)

    return x2d.reshape(batch, seq, d)
```

```python
import functools
import math

import jax
import jax.numpy as jnp
from jax import lax
from jax.experimental import pallas as pl
from jax.experimental.pallas import tpu as pltpu

F32 = jnp.float32
BF16 = jnp.bfloat16

ATTN_HEAD_DIM = 64
ATTN_V_DIM = 2 * ATTN_HEAD_DIM
CONV_K = 3
ROPE_THETA = 10000.0
NORM_EPS = 1e-6
N_MOD = 6
LAMBDA_INIT = 0.8 - 0.6 * math.exp(-0.3 * 0)
LOG2_E = math.log2(math.e)

VMEM_LIMIT_BYTES = 60 * 1024 * 1024
LANES = 128
F32_SUBLANES = 8
BF16_SUBLANES = 16


def _params(*semantics):
    return pltpu.CompilerParams(dimension_semantics=semantics,
                                vmem_limit_bytes=VMEM_LIMIT_BYTES)


def _silu(v):
    return v * jax.nn.sigmoid(v)


def _ada_tile(c_ref, w_ref, b_ref):
    c_act = _silu(c_ref[...]).astype(BF16)
    return jnp.dot(c_act, w_ref[...].astype(BF16), preferred_element_type=F32) + b_ref[...]


def _ada_kernel(c_ref, w_ref, b_ref, o_ref):
    o_ref[...] = _ada_tile(c_ref, w_ref, b_ref)


def _ada(c_pad, w_ada, b_ada, n_cols, tn=512):
    rows, d = c_pad.shape
    return pl.pallas_call(
        _ada_kernel,
        grid=(n_cols // tn,),
        in_specs=[pl.BlockSpec((rows, d), lambda j: (0, 0)),
                  pl.BlockSpec((d, tn), lambda j: (0, j)),
                  pl.BlockSpec((1, tn), lambda j: (0, j))],
        out_specs=pl.BlockSpec((rows, tn), lambda j: (0, j)),
        out_shape=jax.ShapeDtypeStruct((rows, n_cols), F32),
        compiler_params=_params("arbitrary"),
        name="ada_mod",
    )(c_pad, w_ada, b_ada)


def _rms(x):
    return x * lax.rsqrt(jnp.mean(x * x, axis=-1, keepdims=True) + NORM_EPS)


def _for_row_groups(n_rows, group, body):
    for start in range(0, n_rows, group):
        body(pl.ds(start, group))


def _norm_mod_rows(x_ref, o_ref, o_row0, n_rows, gain, scale, shift):
    mult = 1.0 + scale

    def body(rows):
        y = _rms(x_ref[rows, :]) * gain
        o_ref[pl.ds(o_row0 + rows.start, rows.size), :] = (y * mult + shift).astype(o_ref.dtype)

    _for_row_groups(n_rows, BF16_SUBLANES, body)


def _split_blocks(shape, n_steps):
    rows, cols = shape
    for rb in range(n_steps, 0, -1):
        cb = n_steps // rb
        if (rb * cb == n_steps and rows % (rb * BF16_SUBLANES) == 0
                and cols % (cb * LANES) == 0):
            return rb, cb
    raise ValueError(f"cannot split {shape} into {n_steps} blocks")


def _cast_specs(weights, n_steps, step_of):
    specs = []
    for w in weights:
        rb, cb = _split_blocks(w.shape, n_steps)
        specs.append(pl.BlockSpec(
            (w.shape[0] // rb, w.shape[1] // cb),
            lambda *g, cb=cb: (step_of(*g) // cb, step_of(*g) % cb)))
    return specs


def _in_proj_kernel(x_ref, gain_ref, sc_ref, sh_ref, w_ref, *rest, chunk, n_cast):
    cast_src, o_ref, cast_dst = rest[:n_cast], rest[n_cast], rest[n_cast + 1:2 * n_cast + 1]
    h_even, h_odd = rest[2 * n_cast + 1:]
    i, j = pl.program_id(0), pl.program_id(1)

    def norm_chunk(h_write):
        _norm_mod_rows(x_ref, h_write, j * chunk, chunk,
                       gain_ref[...], sc_ref[...], sh_ref[...])

    def step(h_write, h_read):
        norm_chunk(h_write)
        for src, dst in zip(cast_src, cast_dst):
            dst[...] = src[...].astype(dst.dtype)
        o_ref[...] = jnp.dot(h_read[...], w_ref[...].astype(BF16),
                             preferred_element_type=F32).astype(o_ref.dtype)

    @pl.when(i == 0)
    def _():
        norm_chunk(h_even)

    @pl.when(jnp.logical_and(i > 0, i % 2 == 0))
    def _():
        step(h_even, h_odd)

    @pl.when(i % 2 == 1)
    def _():
        step(h_odd, h_even)


def _in_proj(x2d, gain, mod4, sc_idx, sh_idx, w, cast_weights, seq, tm=1024, tn=768):
    m, d = x2d.shape
    n = w.shape[1]
    n_blocks, n_tiles = m // tm, n // tn
    chunk = tm // n_tiles
    assert chunk * n_tiles == tm and chunk % BF16_SUBLANES == 0
    per_batch = seq // tm

    def norm_block(i):
        return jnp.minimum(i, n_blocks - 1)

    def tile(i, j):
        return j * jnp.minimum(i, 1)

    def mult_step(i, j):
        return jnp.maximum(i - 1, 0) * n_tiles + tile(i, j)

    def mod_spec(idx):
        return pl.BlockSpec((None, None, 1, d),
                            lambda i, j: (norm_block(i) // per_batch, idx, 0, 0))

    cast_specs = _cast_specs(cast_weights, n_blocks * n_tiles, mult_step)
    outs = pl.pallas_call(
        functools.partial(_in_proj_kernel, chunk=chunk, n_cast=len(cast_weights)),
        grid=(n_blocks + 1, n_tiles),
        in_specs=[pl.BlockSpec((chunk, d), lambda i, j: (norm_block(i) * n_tiles + j, 0)),
                  pl.BlockSpec((1, d), lambda i, j: (0, 0)),
                  mod_spec(sc_idx), mod_spec(sh_idx),
                  pl.BlockSpec((d, tn), lambda i, j: (0, tile(i, j)))] + cast_specs,
        out_specs=[pl.BlockSpec((tm, tn), lambda i, j: (jnp.maximum(i - 1, 0), tile(i, j)))]
        + cast_specs,
        out_shape=[jax.ShapeDtypeStruct((m, n), BF16)]
        + [jax.ShapeDtypeStruct(cw.shape, BF16) for cw in cast_weights],
        scratch_shapes=[pltpu.VMEM((tm, d), BF16), pltpu.VMEM((tm, d), BF16)],
        compiler_params=_params("arbitrary", "arbitrary"),
        name="in_proj",
    )(x2d, gain, mod4, mod4, w, *cast_weights)
    return outs[0], outs[1:]


def _reduce_keys(t, op, group=64):
    keys, queries = t.shape
    partial = op(t.reshape(keys // group, group, queries), axis=0)
    return op(partial, axis=0, keepdims=True)


def _attn_kernel(q_ref, k_ref, v_ref, cos_ref, sa_ref, sb_ref, lq1_ref, lk1_ref,
                 lq2_ref, lk2_ref, g_ref, c_ref, wada_ref, bada_ref, *rest, tq, n_cast):
    cast_src, o_ref, mod_ref = rest[:n_cast], rest[n_cast], rest[n_cast + 1]
    cast_dst = rest[n_cast + 2:2 * n_cast + 2]
    q1_s, q2_s, k_s, vt_s = rest[2 * n_cast + 2:]
    for src, dst in zip(cast_src, cast_dst):
        dst[...] = src[...].astype(dst.dtype)
    mod_ref[...] = _ada_tile(c_ref, wada_ref, bada_ref)
    seq, width = q_ref.shape
    half = ATTN_HEAD_DIM // 2
    cos, sa, sb = cos_ref[...], sa_ref[...], sb_ref[...]

    def rope(t):
        return (t * cos + pltpu.roll(t, width - half, 1) * sa
                + pltpu.roll(t, half, 1) * sb)

    lane = lax.broadcasted_iota(jnp.int32, (seq, width), 1)
    first = lane < ATTN_HEAD_DIM
    q = rope(q_ref[...].astype(F32)) * (ATTN_HEAD_DIM ** -0.5 * LOG2_E)
    q1_s[...] = jnp.where(first, q, 0.0).astype(BF16)
    q2_s[...] = jnp.where(first, 0.0, q).astype(BF16)
    k_s[...] = rope(k_ref[...].astype(F32)).astype(BF16)
    vt_s[...] = v_ref[...].astype(F32).T.astype(BF16)

    lam = (jnp.exp(jnp.sum(lq1_ref[...] * lk1_ref[...], keepdims=True))
           - jnp.exp(jnp.sum(lq2_ref[...] * lk2_ref[...], keepdims=True))
           + LAMBDA_INIT)
    neg = jnp.finfo(F32).min
    gain = g_ref[...] * (1.0 - LAMBDA_INIT)

    hq = tq // 2
    contract_last = (((1,), (1,)), ((), ()))
    for qi in range(seq // tq):
        r0 = qi * tq
        mid, kv_len = r0 + hq, r0 + tq
        qz = jnp.concatenate([q1_s[r0:mid, :], q2_s[r0:mid, :],
                              q1_s[mid:kv_len, :], q2_s[mid:kv_len, :]], axis=0)
        key = lax.broadcasted_iota(jnp.int32, (mid, 2 * tq), 0)
        col = lax.broadcasted_iota(jnp.int32, (mid, 2 * tq), 1)
        query = r0 + jnp.where(col < hq, col, jnp.where(col < tq, col - hq, tq))
        s_top = lax.dot_general(k_s[0:mid, :], qz, contract_last,
                                preferred_element_type=F32)
        s_top = jnp.where(key <= query, s_top, neg)
        key_b = mid + lax.broadcasted_iota(jnp.int32, (hq, tq), 0)
        col_b = lax.broadcasted_iota(jnp.int32, (hq, tq), 1)
        query_b = mid + jnp.where(col_b < hq, col_b, col_b - hq)
        s_bot = lax.dot_general(k_s[mid:kv_len, :], qz[tq:, :], contract_last,
                                preferred_element_type=F32)
        s_bot = jnp.where(key_b <= query_b, s_bot, neg)
        m_top = _reduce_keys(s_top, jnp.max)
        m = jnp.concatenate(
            [m_top[:, :tq], jnp.maximum(m_top[:, tq:], _reduce_keys(s_bot, jnp.max))], axis=1)
        e_top = jnp.exp2(s_top - m)
        e_bot = jnp.exp2(s_bot - m[:, tq:])
        l_top = _reduce_keys(e_top, jnp.sum)
        denom = jnp.concatenate(
            [l_top[:, :tq], l_top[:, tq:] + _reduce_keys(e_bot, jnp.sum)], axis=1)
        ov = jnp.dot(vt_s[:, 0:mid], e_top.astype(BF16), preferred_element_type=F32)
        ov_b = ov[:, tq:] + jnp.dot(vt_s[:, mid:kv_len], e_bot.astype(BF16),
                                    preferred_element_type=F32)
        ov = jnp.concatenate([ov[:, :tq], ov_b], axis=1) * (1.0 / denom)
        for half in range(2):
            c0 = half * tq
            o = ov[:, c0:c0 + hq] - lam * ov[:, c0 + hq:c0 + tq]
            o = o * lax.rsqrt(jnp.mean(o * o, axis=0, keepdims=True) + NORM_EPS)
            rows = pl.ds(r0 + half * hq, hq)
            o_ref[rows, :] = (o.T * gain).astype(o_ref.dtype)


def _attention(proj, tables, lams, subln_g, ada, cast_weights, batch, seq, n_heads, tq=512):
    dv = ATTN_V_DIM
    cos, sa, sb = tables
    n_steps = batch * n_heads
    full = lambda b, h: (0, 0)
    vec = pl.BlockSpec((1, ATTN_HEAD_DIM), full)
    cast_specs = _cast_specs(cast_weights, n_steps, lambda b, h: b * n_heads + h)
    n_cast = len(cast_weights)
    c_pad, w_ada, b_ada, col0 = ada
    rows, d = c_pad.shape
    mod_cols = w_ada.shape[1] - col0
    tn = mod_cols // n_steps
    assert tn * n_steps == mod_cols and tn % LANES == 0 and col0 % tn == 0
    ada_tile = lambda b, h: (0, col0 // tn + b * n_heads + h)
    outs = pl.pallas_call(
        functools.partial(_attn_kernel, tq=tq, n_cast=n_cast),
        grid=(batch, n_heads),
        in_specs=[pl.BlockSpec((seq, dv), lambda b, h: (b, h)),
                  pl.BlockSpec((seq, dv), lambda b, h: (b, n_heads + h)),
                  pl.BlockSpec((seq, dv), lambda b, h: (b, 2 * n_heads + h)),
                  pl.BlockSpec((seq, dv), full), pl.BlockSpec((seq, dv), full),
                  pl.BlockSpec((seq, dv), full),
                  vec, vec, vec, vec,
                  pl.BlockSpec((1, dv), full),
                  pl.BlockSpec((rows, d), full), pl.BlockSpec((d, tn), ada_tile),
                  pl.BlockSpec((1, tn), ada_tile)] + cast_specs,
        out_specs=[pl.BlockSpec((seq, dv), lambda b, h: (b, h)),
                   pl.BlockSpec((rows, tn), lambda b, h: (0, b * n_heads + h))] + cast_specs,
        out_shape=[jax.ShapeDtypeStruct((batch * seq, n_heads * dv), BF16),
                   jax.ShapeDtypeStruct((rows, mod_cols), F32)]
        + [jax.ShapeDtypeStruct(w.shape, BF16) for w in cast_weights],
        scratch_shapes=[pltpu.VMEM((seq, dv), BF16)] * 3 + [pltpu.VMEM((dv, seq), BF16)],
        compiler_params=_params("arbitrary", "arbitrary"),
        name="diff_attn",
    )(proj, proj, proj, cos, sa, sb, *lams, subln_g, c_pad, w_ada, b_ada, *cast_weights)
    return outs[0], outs[1], outs[2:]


def _conv_kernel(bg_ref, cg_ref, xg_ref, w_ref, o_ref):
    u = cg_ref[...].astype(F32) * xg_ref[...].astype(F32)
    w = w_ref[...]
    row = lax.broadcasted_iota(jnp.int32, u.shape, 0)
    y = w[CONV_K - 1:CONV_K, :] * u
    for shift in range(1, CONV_K):
        shifted = jnp.where(row >= shift, pltpu.roll(u, shift, 0), 0.0)
        y = y + w[CONV_K - 1 - shift:CONV_K - shift, :] * shifted
    o_ref[...] = (bg_ref[...].astype(F32) * y).astype(o_ref.dtype)


def _conv(proj, conv_w, batch, seq, attn_width, conv_width, tc=512):
    qkv_blocks = 3 * attn_width // tc
    cblocks = conv_width // tc
    return pl.pallas_call(
        _conv_kernel,
        grid=(batch, cblocks),
        in_specs=[pl.BlockSpec((seq, tc), lambda b, j: (b, qkv_blocks + j)),
                  pl.BlockSpec((seq, tc), lambda b, j: (b, qkv_blocks + cblocks + j)),
                  pl.BlockSpec((seq, tc), lambda b, j: (b, qkv_blocks + 2 * cblocks + j)),
                  pl.BlockSpec((CONV_K, tc), lambda b, j: (0, j))],
        out_specs=pl.BlockSpec((seq, tc), lambda b, j: (b, j)),
        out_shape=jax.ShapeDtypeStruct((batch * seq, conv_width), BF16),
        compiler_params=_params("arbitrary", "arbitrary"),
        name="gated_conv",
    )(proj, proj, proj, conv_w)


def _out_proj_kernel(attn_ref, conv_ref, wa_ref, wc_ref, x_ref, g_ref, o_ref):
    acc = jnp.dot(attn_ref[...], wa_ref[...], preferred_element_type=F32)
    acc += jnp.dot(conv_ref[...], wc_ref[...], preferred_element_type=F32)
    o_ref[...] = x_ref[...] + g_ref[...] * acc


def _out_proj(attn, conv, w_out, x2d, mod4, gate_idx, seq, tm=1024, tn=1024):
    m, ka = attn.shape
    kc = conv.shape[1]
    assert ka % kc == 0
    n = w_out.shape[1]
    per_batch = seq // tm
    return pl.pallas_call(
        _out_proj_kernel,
        grid=(m // tm, n // tn),
        in_specs=[pl.BlockSpec((tm, ka), lambda i, j: (i, 0)),
                  pl.BlockSpec((tm, kc), lambda i, j: (i, 0)),
                  pl.BlockSpec((ka, tn), lambda i, j: (0, j)),
                  pl.BlockSpec((kc, tn), lambda i, j: (ka // kc, j)),
                  pl.BlockSpec((tm, tn), lambda i, j: (i, j)),
                  pl.BlockSpec((None, None, 1, tn),
                               lambda i, j: (i // per_batch, gate_idx, 0, j))],
        out_specs=pl.BlockSpec((tm, tn), lambda i, j: (i, j)),
        out_shape=jax.ShapeDtypeStruct((m, n), F32),
        compiler_params=_params("arbitrary", "arbitrary"),
        name="out_proj",
    )(attn, conv, w_out, w_out, x2d, mod4)


def _ffn_kernel(x_hbm, xn_ref, gain_ref, sc_ref, sh_ref, scn_ref, shn_ref, gate_ref,
                fgain_ref, wg_ref, wu_ref, wd_ref, o_hbm, h_even, h_odd, acc_s,
                x_buf, o_buf, x_sem, o_sem, *, tm, chunk, ahead):
    i, f = pl.program_id(0), pl.program_id(1)
    n_chunks = tm // chunk
    n_ahead = tm // ahead

    def x_copy(c):
        rows = pl.ds(i * tm + c * chunk, chunk)
        return pltpu.make_async_copy(x_hbm.at[rows], x_buf.at[c % 2], x_sem.at[c % 2])

    def o_copy(c):
        rows = pl.ds(i * tm + c * chunk, chunk)
        return pltpu.make_async_copy(o_buf.at[c % 2], o_hbm.at[rows], o_sem.at[c % 2])

    @pl.when(jnp.logical_and(i == 0, f == 0))
    def _():
        x_copy(0).start()
        for c in range(n_chunks):
            if c + 1 < n_chunks:
                x_copy(c + 1).start()
            x_copy(c).wait()
            _norm_mod_rows(x_buf.at[c % 2], h_even, c * chunk, chunk,
                           gain_ref[...], sc_ref[...], sh_ref[...])

    @pl.when(f == 0)
    def _():
        acc_s[...] = jnp.zeros_like(acc_s)

    def step(h_cur, h_next):
        _norm_mod_rows(xn_ref, h_next, jnp.minimum(f, n_ahead - 1) * ahead, ahead,
                       gain_ref[...], scn_ref[...], shn_ref[...])
        h = h_cur[...]
        g = jnp.dot(h, wg_ref[...], preferred_element_type=F32)
        u = jnp.dot(h, wu_ref[...], preferred_element_type=F32)
        a = (_silu(g) * u).astype(BF16)
        acc_s[...] += jnp.dot(a, wd_ref[...], preferred_element_type=F32)

    @pl.when(i % 2 == 0)
    def _():
        step(h_even, h_odd)

    @pl.when(i % 2 == 1)
    def _():
        step(h_odd, h_even)

    @pl.when(f == pl.num_programs(1) - 1)
    def _():
        x_copy(0).start()
        for c in range(n_chunks):
            if c + 1 < n_chunks:
                x_copy(c + 1).start()
            x_copy(c).wait()
            if c >= 2:
                o_copy(c - 2).wait()
            x_c, o_c = x_buf.at[c % 2], o_buf.at[c % 2]

            def residual_norm(rows, c=c, x_c=x_c, o_c=o_c):
                acc = acc_s[pl.ds(c * chunk + rows.start, rows.size), :]
                o_c[rows, :] = _rms(x_c[rows, :] + gate_ref[...] * acc) * fgain_ref[...]

            _for_row_groups(chunk, F32_SUBLANES, residual_norm)
            o_copy(c).start()
        for c in range(max(n_chunks - 2, 0), n_chunks):
            o_copy(c).wait()


def _ffn(x1, gain, mod4, sc_idx, sh_idx, gate_idx, final_g, w_gate, w_up, w_down, seq,
         tm=1024, tf=256, chunk=128):
    m, d = x1.shape
    d_ff = w_gate.shape[1]
    n_blocks, n_steps = m // tm, d_ff // tf
    per_batch = seq // tm
    n_ahead = 1
    while n_ahead * 2 <= n_steps and tm % (n_ahead * 2 * BF16_SUBLANES) == 0:
        n_ahead *= 2
    ahead = tm // n_ahead
    row = pl.BlockSpec((1, d), lambda i, f: (0, 0))

    def next_block(i):
        return jnp.minimum(i + 1, n_blocks - 1)

    def mod_spec(idx, block=lambda i: i):
        return pl.BlockSpec((None, None, 1, d),
                            lambda i, f: (block(i) // per_batch, idx, 0, 0))

    return pl.pallas_call(
        functools.partial(_ffn_kernel, tm=tm, chunk=chunk, ahead=ahead),
        grid=(n_blocks, n_steps),
        in_specs=[pl.BlockSpec(memory_space=pl.ANY),
                  pl.BlockSpec((ahead, d), lambda i, f: (
                      next_block(i) * n_ahead + jnp.minimum(f, n_ahead - 1), 0)),
                  row, mod_spec(sc_idx), mod_spec(sh_idx),
                  mod_spec(sc_idx, next_block), mod_spec(sh_idx, next_block),
                  mod_spec(gate_idx), row,
                  pl.BlockSpec((d, tf), lambda i, f: (0, f)),
                  pl.BlockSpec((d, tf), lambda i, f: (0, f)),
                  pl.BlockSpec((tf, d), lambda i, f: (f, 0))],
        out_specs=pl.BlockSpec(memory_space=pl.ANY),
        out_shape=jax.ShapeDtypeStruct((m, d), F32),
        scratch_shapes=[pltpu.VMEM((tm, d), BF16), pltpu.VMEM((tm, d), BF16),
                        pltpu.VMEM((tm, d), F32),
                        pltpu.VMEM((2, chunk, d), F32), pltpu.VMEM((2, chunk, d), F32),
                        pltpu.SemaphoreType.DMA((2,)), pltpu.SemaphoreType.DMA((2,))],
        compiler_params=_params("arbitrary", "arbitrary"),
        name="swiglu",
    )(x1, x1, gain, mod4, mod4, mod4, mod4, mod4, final_g, w_gate, w_up, w_down)


def _rope_tables(seq):
    half = ATTN_HEAD_DIM // 2
    inv_freq = ROPE_THETA ** (-jnp.arange(half, dtype=F32) / half)
    ang = jnp.arange(seq, dtype=F32)[:, None] * inv_freq[None, :]
    cos, sin, zero = jnp.cos(ang), jnp.sin(ang), jnp.zeros_like(ang)
    reps = ATTN_V_DIM // ATTN_HEAD_DIM
    cos_t = jnp.tile(cos, (1, 2 * reps))
    sin_lo = jnp.tile(jnp.concatenate([-sin, zero], axis=1), (1, reps))
    sin_hi = jnp.tile(jnp.concatenate([zero, sin], axis=1), (1, reps))
    return cos_t, sin_lo, sin_hi


def kernel(x, c, w_ada, b_ada, norm1_g, w_in, lambda_q1, lambda_k1, lambda_q2, lambda_k2,
           subln_g, conv_w, w_out, norm2_g, w_gate, w_up, w_down, final_g):
    batch, seq, d = x.shape
    depth = w_ada.shape[0]
    assert depth == 1
    conv_width = conv_w.shape[-1]
    mix_width = w_out.shape[1]
    attn_width = mix_width - conv_width
    n_heads = attn_width // ATTN_V_DIM
    x2d = x.reshape(batch * seq, d)
    tables = _rope_tables(seq)

    for l in range(depth):
        c_pad = jnp.pad(c, ((0, F32_SUBLANES - batch), (0, 0)))
        b_row = b_ada[l][None, :]
        n_early = 2
        mod_a = _ada(c_pad, w_ada[l], b_row, n_early * d)
        mod_a = mod_a[:batch].reshape(batch, n_early, 1, d)

        proj, (w_out_b, w_down_b) = _in_proj(x2d, norm1_g[l][None, :], mod_a, 1, 0, w_in[l],
                                             (w_out[l], w_down[l]), seq)
        lams = [v[l][None, :] for v in (lambda_q1, lambda_k1, lambda_q2, lambda_k2)]
        attn, mod_b, (w_gate_b, w_up_b) = _attention(
            proj, tables, lams, subln_g[l][None, :], (c_pad, w_ada[l], b_row, n_early * d),
            (w_gate[l], w_up[l]), batch, seq, n_heads)
        mod_b = mod_b[:batch].reshape(batch, N_MOD - n_early, 1, d)
        conv = _conv(proj, conv_w[l], batch, seq, attn_width, conv_width)
        x2d = _out_proj(attn, conv, w_out_b, x2d, mod_b, 0, seq)

        x2d = _ffn(x2d, norm2_g[l][None, :], mod_b, 2, 1, 3, final_g[None, :],
                   w_gate_b, w_up_b, w_down_b, seq)

    return x2d.reshape(batch, seq, d)
```

```python
import functools
import math

import jax
import jax.numpy as jnp
from jax import lax
from jax.experimental import pallas as pl
from jax.experimental.pallas import tpu as pltpu

F32 = jnp.float32
BF16 = jnp.bfloat16

ATTN_HEAD_DIM = 64
ATTN_V_DIM = 2 * ATTN_HEAD_DIM
CONV_K = 3
ROPE_THETA = 10000.0
NORM_EPS = 1e-6
N_MOD = 6
LAMBDA_INIT = 0.8 - 0.6 * math.exp(-0.3 * 0)
LOG2_E = math.log2(math.e)

VMEM_LIMIT_BYTES = 60 * 1024 * 1024
LANES = 128
F32_SUBLANES = 8
BF16_SUBLANES = 16


def _params(*semantics):
    return pltpu.CompilerParams(dimension_semantics=semantics,
                                vmem_limit_bytes=VMEM_LIMIT_BYTES)


def _silu(v):
    return v * jax.nn.sigmoid(v)


def _ada_tile(c_ref, w_ref, b_ref):
    c_act = _silu(c_ref[...]).astype(BF16)
    return jnp.dot(c_act, w_ref[...].astype(BF16), preferred_element_type=F32) + b_ref[...]


def _ada_kernel(c_ref, w_ref, b_ref, o_ref):
    o_ref[...] = _ada_tile(c_ref, w_ref, b_ref)


def _ada(c_pad, w_ada, b_ada, n_cols, tn=512):
    rows, d = c_pad.shape
    return pl.pallas_call(
        _ada_kernel,
        grid=(n_cols // tn,),
        in_specs=[pl.BlockSpec((rows, d), lambda j: (0, 0)),
                  pl.BlockSpec((d, tn), lambda j: (0, j)),
                  pl.BlockSpec((1, tn), lambda j: (0, j))],
        out_specs=pl.BlockSpec((rows, tn), lambda j: (0, j)),
        out_shape=jax.ShapeDtypeStruct((rows, n_cols), F32),
        compiler_params=_params("arbitrary"),
        name="ada_mod",
    )(c_pad, w_ada, b_ada)


def _rms(x):
    return x * lax.rsqrt(jnp.mean(x * x, axis=-1, keepdims=True) + NORM_EPS)


def _for_row_groups(n_rows, group, body):
    for start in range(0, n_rows, group):
        body(pl.ds(start, group))


def _norm_mod_rows(x_ref, o_ref, o_row0, n_rows, gain, scale, shift):
    mult = 1.0 + scale

    def body(rows):
        y = _rms(x_ref[rows, :]) * gain
        o_ref[pl.ds(o_row0 + rows.start, rows.size), :] = (y * mult + shift).astype(o_ref.dtype)

    _for_row_groups(n_rows, BF16_SUBLANES, body)


def _split_blocks(shape, n_steps):
    rows, cols = shape
    for rb in range(n_steps, 0, -1):
        cb = n_steps // rb
        if (rb * cb == n_steps and rows % (rb * BF16_SUBLANES) == 0
                and cols % (cb * LANES) == 0):
            return rb, cb
    raise ValueError(f"cannot split {shape} into {n_steps} blocks")


def _cast_specs(weights, n_steps, step_of):
    specs = []
    for w in weights:
        rb, cb = _split_blocks(w.shape, n_steps)
        specs.append(pl.BlockSpec(
            (w.shape[0] // rb, w.shape[1] // cb),
            lambda *g, cb=cb: (step_of(*g) // cb, step_of(*g) % cb)))
    return specs


def _in_proj_kernel(x_ref, gain_ref, sc_ref, sh_ref, w_ref, *rest, chunk, n_cast):
    cast_src, o_ref, cast_dst = rest[:n_cast], rest[n_cast], rest[n_cast + 1:2 * n_cast + 1]
    h_even, h_odd = rest[2 * n_cast + 1:]
    i, j = pl.program_id(0), pl.program_id(1)

    def norm_chunk(h_write):
        _norm_mod_rows(x_ref, h_write, j * chunk, chunk,
                       gain_ref[...], sc_ref[...], sh_ref[...])

    def step(h_write, h_read):
        norm_chunk(h_write)
        for src, dst in zip(cast_src, cast_dst):
            dst[...] = src[...].astype(dst.dtype)
        o_ref[...] = jnp.dot(h_read[...], w_ref[...].astype(BF16),
                             preferred_element_type=F32).astype(o_ref.dtype)

    @pl.when(i == 0)
    def _():
        norm_chunk(h_even)

    @pl.when(jnp.logical_and(i > 0, i % 2 == 0))
    def _():
        step(h_even, h_odd)

    @pl.when(i % 2 == 1)
    def _():
        step(h_odd, h_even)


def _in_proj(x2d, gain, mod4, sc_idx, sh_idx, w, cast_weights, seq, tm=1024, tn=768):
    m, d = x2d.shape
    n = w.shape[1]
    n_blocks, n_tiles = m // tm, n // tn
    chunk = tm // n_tiles
    assert chunk * n_tiles == tm and chunk % BF16_SUBLANES == 0
    per_batch = seq // tm

    def norm_block(i):
        return jnp.minimum(i, n_blocks - 1)

    def tile(i, j):
        return j * jnp.minimum(i, 1)

    def mult_step(i, j):
        return jnp.maximum(i - 1, 0) * n_tiles + tile(i, j)

    def mod_spec(idx):
        return pl.BlockSpec((None, None, 1, d),
                            lambda i, j: (norm_block(i) // per_batch, idx, 0, 0))

    cast_specs = _cast_specs(cast_weights, n_blocks * n_tiles, mult_step)
    outs = pl.pallas_call(
        functools.partial(_in_proj_kernel, chunk=chunk, n_cast=len(cast_weights)),
        grid=(n_blocks + 1, n_tiles),
        in_specs=[pl.BlockSpec((chunk, d), lambda i, j: (norm_block(i) * n_tiles + j, 0)),
                  pl.BlockSpec((1, d), lambda i, j: (0, 0)),
                  mod_spec(sc_idx), mod_spec(sh_idx),
                  pl.BlockSpec((d, tn), lambda i, j: (0, tile(i, j)))] + cast_specs,
        out_specs=[pl.BlockSpec((tm, tn), lambda i, j: (jnp.maximum(i - 1, 0), tile(i, j)))]
        + cast_specs,
        out_shape=[jax.ShapeDtypeStruct((m, n), BF16)]
        + [jax.ShapeDtypeStruct(cw.shape, BF16) for cw in cast_weights],
        scratch_shapes=[pltpu.VMEM((tm, d), BF16), pltpu.VMEM((tm, d), BF16)],
        compiler_params=_params("arbitrary", "arbitrary"),
        name="in_proj",
    )(x2d, gain, mod4, mod4, w, *cast_weights)
    return outs[0], outs[1:]


def _reduce_keys(t, op, group=64):
    keys, queries = t.shape
    partial = op(t.reshape(keys // group, group, queries), axis=0)
    return op(partial, axis=0, keepdims=True)


def _attn_kernel(q_ref, k_ref, v_ref, cos_ref, sa_ref, sb_ref, lq1_ref, lk1_ref,
                 lq2_ref, lk2_ref, g_ref, c_ref, wada_ref, bada_ref, *rest, tq, n_cast):
    cast_src, o_ref, mod_ref = rest[:n_cast], rest[n_cast], rest[n_cast + 1]
    cast_dst = rest[n_cast + 2:2 * n_cast + 2]
    q1_s, q2_s, k_s, vt_s = rest[2 * n_cast + 2:]
    for src, dst in zip(cast_src, cast_dst):
        dst[...] = src[...].astype(dst.dtype)
    mod_ref[...] = _ada_tile(c_ref, wada_ref, bada_ref)
    seq, width = q_ref.shape
    half = ATTN_HEAD_DIM // 2
    cos, sa, sb = cos_ref[...], sa_ref[...], sb_ref[...]

    def rope(t):
        return (t * cos + pltpu.roll(t, width - half, 1) * sa
                + pltpu.roll(t, half, 1) * sb)

    lane = lax.broadcasted_iota(jnp.int32, (seq, width), 1)
    first = lane < ATTN_HEAD_DIM
    q = rope(q_ref[...].astype(F32)) * (ATTN_HEAD_DIM ** -0.5 * LOG2_E)
    q1_s[...] = jnp.where(first, q, 0.0).astype(BF16)
    q2_s[...] = jnp.where(first, 0.0, q).astype(BF16)
    k_s[...] = rope(k_ref[...].astype(F32)).astype(BF16)
    vt_s[0:width, :] = v_ref[...].astype(F32).T.astype(BF16)
    vt_s[width:, :] = jnp.ones((BF16_SUBLANES, seq), BF16)

    lam = (jnp.exp(jnp.sum(lq1_ref[...] * lk1_ref[...], keepdims=True))
           - jnp.exp(jnp.sum(lq2_ref[...] * lk2_ref[...], keepdims=True))
           + LAMBDA_INIT)
    neg = jnp.finfo(F32).min
    gain = g_ref[...] * (1.0 - LAMBDA_INIT)

    hq = tq // 2
    contract_last = (((1,), (1,)), ((), ()))
    for qi in range(seq // tq):
        r0 = qi * tq
        mid, kv_len = r0 + hq, r0 + tq
        qz = jnp.concatenate([q1_s[r0:mid, :], q2_s[r0:mid, :],
                              q1_s[mid:kv_len, :], q2_s[mid:kv_len, :]], axis=0)
        key = lax.broadcasted_iota(jnp.int32, (mid, 2 * tq), 0)
        col = lax.broadcasted_iota(jnp.int32, (mid, 2 * tq), 1)
        query = r0 + jnp.where(col < hq, col, jnp.where(col < tq, col - hq, tq))
        s_top = lax.dot_general(k_s[0:mid, :], qz, contract_last,
                                preferred_element_type=F32)
        s_top = jnp.where(key <= query, s_top, neg)
        key_b = mid + lax.broadcasted_iota(jnp.int32, (hq, tq), 0)
        col_b = lax.broadcasted_iota(jnp.int32, (hq, tq), 1)
        query_b = mid + jnp.where(col_b < hq, col_b, col_b - hq)
        s_bot = lax.dot_general(k_s[mid:kv_len, :], qz[tq:, :], contract_last,
                                preferred_element_type=F32)
        s_bot = jnp.where(key_b <= query_b, s_bot, neg)
        m_top = _reduce_keys(s_top, jnp.max)
        m = jnp.concatenate(
            [m_top[:, :tq], jnp.maximum(m_top[:, tq:], _reduce_keys(s_bot, jnp.max))], axis=1)
        e_top = jnp.exp2(s_top - m).astype(BF16)
        e_bot = jnp.exp2(s_bot - m[:, tq:]).astype(BF16)
        ov = jnp.dot(vt_s[:, 0:mid], e_top, preferred_element_type=F32)
        ov_b = ov[:, tq:] + jnp.dot(vt_s[:, mid:kv_len], e_bot, preferred_element_type=F32)
        ov = jnp.concatenate([ov[:, :tq], ov_b], axis=1)
        ov = ov[:width, :] * (1.0 / ov[width:width + 1, :])
        for half in range(2):
            c0 = half * tq
            o = ov[:, c0:c0 + hq] - lam * ov[:, c0 + hq:c0 + tq]
            o = o * lax.rsqrt(jnp.mean(o * o, axis=0, keepdims=True) + NORM_EPS)
            rows = pl.ds(r0 + half * hq, hq)
            o_ref[rows, :] = (o.T * gain).astype(o_ref.dtype)


def _attention(proj, tables, lams, subln_g, ada, cast_weights, batch, seq, n_heads, tq=512):
    dv = ATTN_V_DIM
    cos, sa, sb = tables
    n_steps = batch * n_heads
    full = lambda b, h: (0, 0)
    vec = pl.BlockSpec((1, ATTN_HEAD_DIM), full)
    cast_specs = _cast_specs(cast_weights, n_steps, lambda b, h: b * n_heads + h)
    n_cast = len(cast_weights)
    c_pad, w_ada, b_ada, col0 = ada
    rows, d = c_pad.shape
    mod_cols = w_ada.shape[1] - col0
    tn = mod_cols // n_steps
    assert tn * n_steps == mod_cols and tn % LANES == 0 and col0 % tn == 0
    ada_tile = lambda b, h: (0, col0 // tn + b * n_heads + h)
    outs = pl.pallas_call(
        functools.partial(_attn_kernel, tq=tq, n_cast=n_cast),
        grid=(batch, n_heads),
        in_specs=[pl.BlockSpec((seq, dv), lambda b, h: (b, h)),
                  pl.BlockSpec((seq, dv), lambda b, h: (b, n_heads + h)),
                  pl.BlockSpec((seq, dv), lambda b, h: (b, 2 * n_heads + h)),
                  pl.BlockSpec((seq, dv), full), pl.BlockSpec((seq, dv), full),
                  pl.BlockSpec((seq, dv), full),
                  vec, vec, vec, vec,
                  pl.BlockSpec((1, dv), full),
                  pl.BlockSpec((rows, d), full), pl.BlockSpec((d, tn), ada_tile),
                  pl.BlockSpec((1, tn), ada_tile)] + cast_specs,
        out_specs=[pl.BlockSpec((seq, dv), lambda b, h: (b, h)),
                   pl.BlockSpec((rows, tn), lambda b, h: (0, b * n_heads + h))] + cast_specs,
        out_shape=[jax.ShapeDtypeStruct((batch * seq, n_heads * dv), BF16),
                   jax.ShapeDtypeStruct((rows, mod_cols), F32)]
        + [jax.ShapeDtypeStruct(w.shape, BF16) for w in cast_weights],
        scratch_shapes=[pltpu.VMEM((seq, dv), BF16)] * 3
        + [pltpu.VMEM((dv + BF16_SUBLANES, seq), BF16)],
        compiler_params=_params("arbitrary", "arbitrary"),
        name="diff_attn",
    )(proj, proj, proj, cos, sa, sb, *lams, subln_g, c_pad, w_ada, b_ada, *cast_weights)
    return outs[0], outs[1], outs[2:]


def _conv_kernel(bg_ref, cg_ref, xg_ref, w_ref, o_ref):
    u = cg_ref[...].astype(F32) * xg_ref[...].astype(F32)
    w = w_ref[...]
    row = lax.broadcasted_iota(jnp.int32, u.shape, 0)
    y = w[CONV_K - 1:CONV_K, :] * u
    for shift in range(1, CONV_K):
        shifted = jnp.where(row >= shift, pltpu.roll(u, shift, 0), 0.0)
        y = y + w[CONV_K - 1 - shift:CONV_K - shift, :] * shifted
    o_ref[...] = (bg_ref[...].astype(F32) * y).astype(o_ref.dtype)


def _conv(proj, conv_w, batch, seq, attn_width, conv_width, tc=512):
    qkv_blocks = 3 * attn_width // tc
    cblocks = conv_width // tc
    return pl.pallas_call(
        _conv_kernel,
        grid=(batch, cblocks),
        in_specs=[pl.BlockSpec((seq, tc), lambda b, j: (b, qkv_blocks + j)),
                  pl.BlockSpec((seq, tc), lambda b, j: (b, qkv_blocks + cblocks + j)),
                  pl.BlockSpec((seq, tc), lambda b, j: (b, qkv_blocks + 2 * cblocks + j)),
                  pl.BlockSpec((CONV_K, tc), lambda b, j: (0, j))],
        out_specs=pl.BlockSpec((seq, tc), lambda b, j: (b, j)),
        out_shape=jax.ShapeDtypeStruct((batch * seq, conv_width), BF16),
        compiler_params=_params("arbitrary", "arbitrary"),
        name="gated_conv",
    )(proj, proj, proj, conv_w)


def _out_proj_kernel(attn_ref, conv_ref, wa_ref, wc_ref, x_ref, g_ref, o_ref):
    acc = jnp.dot(attn_ref[...], wa_ref[...], preferred_element_type=F32)
    acc += jnp.dot(conv_ref[...], wc_ref[...], preferred_element_type=F32)
    o_ref[...] = x_ref[...] + g_ref[...] * acc


def _out_proj(attn, conv, w_out, x2d, mod4, gate_idx, seq, tm=1024, tn=1024):
    m, ka = attn.shape
    kc = conv.shape[1]
    assert ka % kc == 0
    n = w_out.shape[1]
    per_batch = seq // tm
    return pl.pallas_call(
        _out_proj_kernel,
        grid=(m // tm, n // tn),
        in_specs=[pl.BlockSpec((tm, ka), lambda i, j: (i, 0)),
                  pl.BlockSpec((tm, kc), lambda i, j: (i, 0)),
                  pl.BlockSpec((ka, tn), lambda i, j: (0, j)),
                  pl.BlockSpec((kc, tn), lambda i, j: (ka // kc, j)),
                  pl.BlockSpec((tm, tn), lambda i, j: (i, j)),
                  pl.BlockSpec((None, None, 1, tn),
                               lambda i, j: (i // per_batch, gate_idx, 0, j))],
        out_specs=pl.BlockSpec((tm, tn), lambda i, j: (i, j)),
        out_shape=jax.ShapeDtypeStruct((m, n), F32),
        compiler_params=_params("arbitrary", "arbitrary"),
        name="out_proj",
    )(attn, conv, w_out, w_out, x2d, mod4)


def _ffn_kernel(x_hbm, xn_ref, gain_ref, sc_ref, sh_ref, scn_ref, shn_ref, gate_ref,
                fgain_ref, wg_ref, wu_ref, wd_ref, o_hbm, h_even, h_odd, acc_s,
                x_buf, o_buf, x_sem, o_sem, *, tm, chunk, ahead):
    i, f = pl.program_id(0), pl.program_id(1)
    n_chunks = tm // chunk
    n_ahead = tm // ahead

    def x_copy(c):
        rows = pl.ds(i * tm + c * chunk, chunk)
        return pltpu.make_async_copy(x_hbm.at[rows], x_buf.at[c % 2], x_sem.at[c % 2])

    def o_copy(c):
        rows = pl.ds(i * tm + c * chunk, chunk)
        return pltpu.make_async_copy(o_buf.at[c % 2], o_hbm.at[rows], o_sem.at[c % 2])

    @pl.when(jnp.logical_and(i == 0, f == 0))
    def _():
        x_copy(0).start()
        for c in range(n_chunks):
            if c + 1 < n_chunks:
                x_copy(c + 1).start()
            x_copy(c).wait()
            _norm_mod_rows(x_buf.at[c % 2], h_even, c * chunk, chunk,
                           gain_ref[...], sc_ref[...], sh_ref[...])

    @pl.when(f == 0)
    def _():
        acc_s[...] = jnp.zeros_like(acc_s)

    def step(h_cur, h_next):
        _norm_mod_rows(xn_ref, h_next, jnp.minimum(f, n_ahead - 1) * ahead, ahead,
                       gain_ref[...], scn_ref[...], shn_ref[...])
        h = h_cur[...]
        g = jnp.dot(h, wg_ref[...], preferred_element_type=F32)
        u = jnp.dot(h, wu_ref[...], preferred_element_type=F32)
        a = (_silu(g) * u).astype(BF16)
        acc_s[...] += jnp.dot(a, wd_ref[...], preferred_element_type=F32)

    @pl.when(i % 2 == 0)
    def _():
        step(h_even, h_odd)

    @pl.when(i % 2 == 1)
    def _():
        step(h_odd, h_even)

    @pl.when(f == pl.num_programs(1) - 1)
    def _():
        x_copy(0).start()
        for c in range(n_chunks):
            if c + 1 < n_chunks:
                x_copy(c + 1).start()
            x_copy(c).wait()
            if c >= 2:
                o_copy(c - 2).wait()
            x_c, o_c = x_buf.at[c % 2], o_buf.at[c % 2]

            def residual_norm(rows, c=c, x_c=x_c, o_c=o_c):
                acc = acc_s[pl.ds(c * chunk + rows.start, rows.size), :]
                o_c[rows, :] = _rms(x_c[rows, :] + gate_ref[...] * acc) * fgain_ref[...]

            _for_row_groups(chunk, F32_SUBLANES, residual_norm)
            o_copy(c).start()
        for c in range(max(n_chunks - 2, 0), n_chunks):
            o_copy(c).wait()


def _ffn(x1, gain, mod4, sc_idx, sh_idx, gate_idx, final_g, w_gate, w_up, w_down, seq,
         tm=1024, tf=256, chunk=128):
    m, d = x1.shape
    d_ff = w_gate.shape[1]
    n_blocks, n_steps = m // tm, d_ff // tf
    per_batch = seq // tm
    n_ahead = 1
    while n_ahead * 2 <= n_steps and tm % (n_ahead * 2 * BF16_SUBLANES) == 0:
        n_ahead *= 2
    ahead = tm // n_ahead
    row = pl.BlockSpec((1, d), lambda i, f: (0, 0))

    def next_block(i):
        return jnp.minimum(i + 1, n_blocks - 1)

    def mod_spec(idx, block=lambda i: i):
        return pl.BlockSpec((None, None, 1, d),
                            lambda i, f: (block(i) // per_batch, idx, 0, 0))

    return pl.pallas_call(
        functools.partial(_ffn_kernel, tm=tm, chunk=chunk, ahead=ahead),
        grid=(n_blocks, n_steps),
        in_specs=[pl.BlockSpec(memory_space=pl.ANY),
                  pl.BlockSpec((ahead, d), lambda i, f: (
                      next_block(i) * n_ahead + jnp.minimum(f, n_ahead - 1), 0)),
                  row, mod_spec(sc_idx), mod_spec(sh_idx),
                  mod_spec(sc_idx, next_block), mod_spec(sh_idx, next_block),
                  mod_spec(gate_idx), row,
                  pl.BlockSpec((d, tf), lambda i, f: (0, f)),
                  pl.BlockSpec((d, tf), lambda i, f: (0, f)),
                  pl.BlockSpec((tf, d), lambda i, f: (f, 0))],
        out_specs=pl.BlockSpec(memory_space=pl.ANY),
        out_shape=jax.ShapeDtypeStruct((m, d), F32),
        scratch_shapes=[pltpu.VMEM((tm, d), BF16), pltpu.VMEM((tm, d), BF16),
                        pltpu.VMEM((tm, d), F32),
                        pltpu.VMEM((2, chunk, d), F32), pltpu.VMEM((2, chunk, d), F32),
                        pltpu.SemaphoreType.DMA((2,)), pltpu.SemaphoreType.DMA((2,))],
        compiler_params=_params("arbitrary", "arbitrary"),
        name="swiglu",
    )(x1, x1, gain, mod4, mod4, mod4, mod4, mod4, final_g, w_gate, w_up, w_down)


def _rope_tables(seq):
    half = ATTN_HEAD_DIM // 2
    inv_freq = ROPE_THETA ** (-jnp.arange(half, dtype=F32) / half)
    ang = jnp.arange(seq, dtype=F32)[:, None] * inv_freq[None, :]
    cos, sin, zero = jnp.cos(ang), jnp.sin(ang), jnp.zeros_like(ang)
    reps = ATTN_V_DIM // ATTN_HEAD_DIM
    cos_t = jnp.tile(cos, (1, 2 * reps))
    sin_lo = jnp.tile(jnp.concatenate([-sin, zero], axis=1), (1, reps))
    sin_hi = jnp.tile(jnp.concatenate([zero, sin], axis=1), (1, reps))
    return cos_t, sin_lo, sin_hi


def kernel(x, c, w_ada, b_ada, norm1_g, w_in, lambda_q1, lambda_k1, lambda_q2, lambda_k2,
           subln_g, conv_w, w_out, norm2_g, w_gate, w_up, w_down, final_g):
    batch, seq, d = x.shape
    depth = w_ada.shape[0]
    assert depth == 1
    conv_width = conv_w.shape[-1]
    mix_width = w_out.shape[1]
    attn_width = mix_width - conv_width
    n_heads = attn_width // ATTN_V_DIM
    x2d = x.reshape(batch * seq, d)
    tables = _rope_tables(seq)

    for l in range(depth):
        c_pad = jnp.pad(c, ((0, F32_SUBLANES - batch), (0, 0)))
        b_row = b_ada[l][None, :]
        n_early = 2
        mod_a = _ada(c_pad, w_ada[l], b_row, n_early * d)
        mod_a = mod_a[:batch].reshape(batch, n_early, 1, d)

        proj, (w_out_b, w_down_b) = _in_proj(x2d, norm1_g[l][None, :], mod_a, 1, 0, w_in[l],
                                             (w_out[l], w_down[l]), seq)
        lams = [v[l][None, :] for v in (lambda_q1, lambda_k1, lambda_q2, lambda_k2)]
        attn, mod_b, (w_gate_b, w_up_b) = _attention(
            proj, tables, lams, subln_g[l][None, :], (c_pad, w_ada[l], b_row, n_early * d),
            (w_gate[l], w_up[l]), batch, seq, n_heads)
        mod_b = mod_b[:batch].reshape(batch, N_MOD - n_early, 1, d)
        conv = _conv(proj, conv_w[l], batch, seq, attn_width, conv_width)
        x2d = _out_proj(attn, conv, w_out_b, x2d, mod_b, 0, seq)

        x2d = _ffn(x2d, norm2_g[l][None, :], mod_b, 2, 1, 3, final_g[None, :],
                   w_gate_b, w_up_b, w_down_b, seq)

    return x2d.reshape(batch, seq, d)
```

```python
import functools
import math

import jax
import jax.numpy as jnp
from jax import lax
from jax.experimental import pallas as pl
from jax.experimental.pallas import tpu as pltpu

F32 = jnp.float32
BF16 = jnp.bfloat16

ATTN_HEAD_DIM = 64
ATTN_V_DIM = 2 * ATTN_HEAD_DIM
CONV_K = 3
ROPE_THETA = 10000.0
NORM_EPS = 1e-6
N_MOD = 6
LAMBDA_INIT = 0.8 - 0.6 * math.exp(-0.3 * 0)
LOG2_E = math.log2(math.e)

VMEM_LIMIT_BYTES = 60 * 1024 * 1024
LANES = 128
F32_SUBLANES = 8
BF16_SUBLANES = 16


def _params(*semantics):
    return pltpu.CompilerParams(dimension_semantics=semantics,
                                vmem_limit_bytes=VMEM_LIMIT_BYTES)


def _silu(v):
    return v * jax.nn.sigmoid(v)


def _ada_tile(c_ref, w_ref, b_ref):
    c_act = _silu(c_ref[...]).astype(BF16)
    return jnp.dot(c_act, w_ref[...].astype(BF16), preferred_element_type=F32) + b_ref[...]


def _ada_kernel(c_ref, w_ref, b_ref, o_ref):
    o_ref[...] = _ada_tile(c_ref, w_ref, b_ref)


def _ada(c_pad, w_ada, b_ada, n_cols, tn=512):
    rows, d = c_pad.shape
    return pl.pallas_call(
        _ada_kernel,
        grid=(n_cols // tn,),
        in_specs=[pl.BlockSpec((rows, d), lambda j: (0, 0)),
                  pl.BlockSpec((d, tn), lambda j: (0, j)),
                  pl.BlockSpec((1, tn), lambda j: (0, j))],
        out_specs=pl.BlockSpec((rows, tn), lambda j: (0, j)),
        out_shape=jax.ShapeDtypeStruct((rows, n_cols), F32),
        compiler_params=_params("arbitrary"),
        name="ada_mod",
    )(c_pad, w_ada, b_ada)


def _rms(x):
    return x * lax.rsqrt(jnp.mean(x * x, axis=-1, keepdims=True) + NORM_EPS)


def _for_row_groups(n_rows, group, body):
    for start in range(0, n_rows, group):
        body(pl.ds(start, group))


def _norm_mod_rows(x_ref, o_ref, o_row0, n_rows, gain, scale, shift):
    mult = 1.0 + scale

    def body(rows):
        y = _rms(x_ref[rows, :]) * gain
        o_ref[pl.ds(o_row0 + rows.start, rows.size), :] = (y * mult + shift).astype(o_ref.dtype)

    _for_row_groups(n_rows, BF16_SUBLANES, body)


def _split_blocks(shape, n_steps):
    rows, cols = shape
    for rb in range(n_steps, 0, -1):
        cb = n_steps // rb
        if (rb * cb == n_steps and rows % (rb * BF16_SUBLANES) == 0
                and cols % (cb * LANES) == 0):
            return rb, cb
    raise ValueError(f"cannot split {shape} into {n_steps} blocks")


def _cast_specs(weights, n_steps, step_of):
    specs = []
    for w in weights:
        rb, cb = _split_blocks(w.shape, n_steps)
        specs.append(pl.BlockSpec(
            (w.shape[0] // rb, w.shape[1] // cb),
            lambda *g, cb=cb: (step_of(*g) // cb, step_of(*g) % cb)))
    return specs


def _in_proj_kernel(x_ref, gain_ref, sc_ref, sh_ref, w_ref, *rest, chunk, n_cast):
    cast_src, o_ref, cast_dst = rest[:n_cast], rest[n_cast], rest[n_cast + 1:2 * n_cast + 1]
    h_even, h_odd = rest[2 * n_cast + 1:]
    i, j = pl.program_id(0), pl.program_id(1)

    def norm_chunk(h_write):
        _norm_mod_rows(x_ref, h_write, j * chunk, chunk,
                       gain_ref[...], sc_ref[...], sh_ref[...])

    def step(h_write, h_read):
        norm_chunk(h_write)
        for src, dst in zip(cast_src, cast_dst):
            dst[...] = src[...].astype(dst.dtype)
        o_ref[...] = jnp.dot(h_read[...], w_ref[...].astype(BF16),
                             preferred_element_type=F32).astype(o_ref.dtype)

    @pl.when(i == 0)
    def _():
        norm_chunk(h_even)

    @pl.when(jnp.logical_and(i > 0, i % 2 == 0))
    def _():
        step(h_even, h_odd)

    @pl.when(i % 2 == 1)
    def _():
        step(h_odd, h_even)


def _in_proj(x2d, gain, mod4, sc_idx, sh_idx, w, cast_weights, seq, tm=1024, tn=768):
    m, d = x2d.shape
    n = w.shape[1]
    n_blocks, n_tiles = m // tm, n // tn
    chunk = tm // n_tiles
    assert chunk * n_tiles == tm and chunk % BF16_SUBLANES == 0
    per_batch = seq // tm

    def norm_block(i):
        return jnp.minimum(i, n_blocks - 1)

    def tile(i, j):
        return j * jnp.minimum(i, 1)

    def mult_step(i, j):
        return jnp.maximum(i - 1, 0) * n_tiles + tile(i, j)

    def mod_spec(idx):
        return pl.BlockSpec((None, None, 1, d),
                            lambda i, j: (norm_block(i) // per_batch, idx, 0, 0))

    cast_specs = _cast_specs(cast_weights, n_blocks * n_tiles, mult_step)
    outs = pl.pallas_call(
        functools.partial(_in_proj_kernel, chunk=chunk, n_cast=len(cast_weights)),
        grid=(n_blocks + 1, n_tiles),
        in_specs=[pl.BlockSpec((chunk, d), lambda i, j: (norm_block(i) * n_tiles + j, 0)),
                  pl.BlockSpec((1, d), lambda i, j: (0, 0)),
                  mod_spec(sc_idx), mod_spec(sh_idx),
                  pl.BlockSpec((d, tn), lambda i, j: (0, tile(i, j)))] + cast_specs,
        out_specs=[pl.BlockSpec((tm, tn), lambda i, j: (jnp.maximum(i - 1, 0), tile(i, j)))]
        + cast_specs,
        out_shape=[jax.ShapeDtypeStruct((m, n), BF16)]
        + [jax.ShapeDtypeStruct(cw.shape, BF16) for cw in cast_weights],
        scratch_shapes=[pltpu.VMEM((tm, d), BF16), pltpu.VMEM((tm, d), BF16)],
        compiler_params=_params("arbitrary", "arbitrary"),
        name="in_proj",
    )(x2d, gain, mod4, mod4, w, *cast_weights)
    return outs[0], outs[1:]


def _reduce_keys(t, op, group=64):
    keys, queries = t.shape
    partial = op(t.reshape(keys // group, group, queries), axis=0)
    return op(partial, axis=0, keepdims=True)


def _attn_kernel(q_ref, k_ref, v_ref, cos_ref, sa_ref, sb_ref, lq1_ref, lk1_ref,
                 lq2_ref, lk2_ref, g_ref, c_ref, wada_ref, bada_ref, *rest, tq, n_cast):
    cast_src, o_ref, mod_ref = rest[:n_cast], rest[n_cast], rest[n_cast + 1]
    cast_dst = rest[n_cast + 2:2 * n_cast + 2]
    q1_s, q2_s, k_s, vt_s = rest[2 * n_cast + 2:]
    for src, dst in zip(cast_src, cast_dst):
        dst[...] = src[...].astype(dst.dtype)
    mod_ref[...] = _ada_tile(c_ref, wada_ref, bada_ref)
    seq, width = q_ref.shape
    half = ATTN_HEAD_DIM // 2
    cos, sa, sb = cos_ref[...], sa_ref[...], sb_ref[...]

    def rope(t):
        return (t * cos + pltpu.roll(t, width - half, 1) * sa
                + pltpu.roll(t, half, 1) * sb)

    lane = lax.broadcasted_iota(jnp.int32, (seq, width), 1)
    first = lane < ATTN_HEAD_DIM
    q = rope(q_ref[...].astype(F32)) * (ATTN_HEAD_DIM ** -0.5 * LOG2_E)
    q1_s[...] = jnp.where(first, q, 0.0).astype(BF16)
    q2_s[...] = jnp.where(first, 0.0, q).astype(BF16)
    k_s[...] = rope(k_ref[...].astype(F32)).astype(BF16)
    vt_s[0:width, :] = v_ref[...].astype(F32).T.astype(BF16)
    vt_s[width:, :] = jnp.ones((BF16_SUBLANES, seq), BF16)

    lam = (jnp.exp(jnp.sum(lq1_ref[...] * lk1_ref[...], keepdims=True))
           - jnp.exp(jnp.sum(lq2_ref[...] * lk2_ref[...], keepdims=True))
           + LAMBDA_INIT)
    neg = jnp.finfo(F32).min
    gain = g_ref[...] * (1.0 - LAMBDA_INIT)

    hq = tq // 2
    contract_last = (((1,), (1,)), ((), ()))
    for qi in range(seq // tq):
        r0 = qi * tq
        mid, kv_len = r0 + hq, r0 + tq
        qz = jnp.concatenate([q1_s[r0:mid, :], q2_s[r0:mid, :],
                              q1_s[mid:kv_len, :], q2_s[mid:kv_len, :]], axis=0)
        key = lax.broadcasted_iota(jnp.int32, (mid, 2 * tq), 0)
        col = lax.broadcasted_iota(jnp.int32, (mid, 2 * tq), 1)
        query = r0 + jnp.where(col < hq, col, jnp.where(col < tq, col - hq, tq))
        s_top = lax.dot_general(k_s[0:mid, :], qz, contract_last,
                                preferred_element_type=F32)
        s_top = jnp.where(key <= query, s_top, neg)
        key_b = mid + lax.broadcasted_iota(jnp.int32, (hq, tq), 0)
        col_b = lax.broadcasted_iota(jnp.int32, (hq, tq), 1)
        query_b = mid + jnp.where(col_b < hq, col_b, col_b - hq)
        s_bot = lax.dot_general(k_s[mid:kv_len, :], qz[tq:, :], contract_last,
                                preferred_element_type=F32)
        s_bot = jnp.where(key_b <= query_b, s_bot, neg)
        m_top = _reduce_keys(s_top, jnp.max)
        m = jnp.concatenate(
            [m_top[:, :tq], jnp.maximum(m_top[:, tq:], _reduce_keys(s_bot, jnp.max))], axis=1)
        e_top = jnp.exp2(s_top - m).astype(BF16)
        e_bot = jnp.exp2(s_bot - m[:, tq:]).astype(BF16)
        ov = jnp.dot(vt_s[:, 0:mid], e_top, preferred_element_type=F32)
        ov_b = ov[:, tq:] + jnp.dot(vt_s[:, mid:kv_len], e_bot, preferred_element_type=F32)
        ov = jnp.concatenate([ov[:, :tq], ov_b], axis=1)
        ov = ov[:width, :] * (1.0 / ov[width:width + 1, :])
        for half in range(2):
            c0 = half * tq
            o = ov[:, c0:c0 + hq] - lam * ov[:, c0 + hq:c0 + tq]
            o = o * lax.rsqrt(jnp.mean(o * o, axis=0, keepdims=True) + NORM_EPS)
            rows = pl.ds(r0 + half * hq, hq)
            o_ref[rows, :] = (o.T * gain).astype(o_ref.dtype)


def _attention(proj, tables, lams, subln_g, ada, cast_weights, batch, seq, n_heads, tq=512):
    dv = ATTN_V_DIM
    cos, sa, sb = tables
    n_steps = batch * n_heads
    full = lambda b, h: (0, 0)
    vec = pl.BlockSpec((1, ATTN_HEAD_DIM), full)
    cast_specs = _cast_specs(cast_weights, n_steps, lambda b, h: b * n_heads + h)
    n_cast = len(cast_weights)
    c_pad, w_ada, b_ada, col0 = ada
    rows, d = c_pad.shape
    mod_cols = w_ada.shape[1] - col0
    tn = mod_cols // n_steps
    assert tn * n_steps == mod_cols and tn % LANES == 0 and col0 % tn == 0
    ada_tile = lambda b, h: (0, col0 // tn + b * n_heads + h)
    outs = pl.pallas_call(
        functools.partial(_attn_kernel, tq=tq, n_cast=n_cast),
        grid=(batch, n_heads),
        in_specs=[pl.BlockSpec((seq, dv), lambda b, h: (b, h)),
                  pl.BlockSpec((seq, dv), lambda b, h: (b, n_heads + h)),
                  pl.BlockSpec((seq, dv), lambda b, h: (b, 2 * n_heads + h)),
                  pl.BlockSpec((seq, dv), full), pl.BlockSpec((seq, dv), full),
                  pl.BlockSpec((seq, dv), full),
                  vec, vec, vec, vec,
                  pl.BlockSpec((1, dv), full),
                  pl.BlockSpec((rows, d), full), pl.BlockSpec((d, tn), ada_tile),
                  pl.BlockSpec((1, tn), ada_tile)] + cast_specs,
        out_specs=[pl.BlockSpec((seq, dv), lambda b, h: (b, h)),
                   pl.BlockSpec((rows, tn), lambda b, h: (0, b * n_heads + h))] + cast_specs,
        out_shape=[jax.ShapeDtypeStruct((batch * seq, n_heads * dv), BF16),
                   jax.ShapeDtypeStruct((rows, mod_cols), F32)]
        + [jax.ShapeDtypeStruct(w.shape, BF16) for w in cast_weights],
        scratch_shapes=[pltpu.VMEM((seq, dv), BF16)] * 3
        + [pltpu.VMEM((dv + BF16_SUBLANES, seq), BF16)],
        compiler_params=_params("arbitrary", "arbitrary"),
        name="diff_attn",
    )(proj, proj, proj, cos, sa, sb, *lams, subln_g, c_pad, w_ada, b_ada, *cast_weights)
    return outs[0], outs[1], outs[2:]


def _conv_kernel(bg_ref, cg_ref, xg_ref, w_ref, o_ref):
    u = cg_ref[...].astype(F32) * xg_ref[...].astype(F32)
    w = w_ref[...]
    row = lax.broadcasted_iota(jnp.int32, u.shape, 0)
    y = w[CONV_K - 1:CONV_K, :] * u
    for shift in range(1, CONV_K):
        shifted = jnp.where(row >= shift, pltpu.roll(u, shift, 0), 0.0)
        y = y + w[CONV_K - 1 - shift:CONV_K - shift, :] * shifted
    o_ref[...] = (bg_ref[...].astype(F32) * y).astype(o_ref.dtype)


def _conv(proj, conv_w, batch, seq, attn_width, conv_width, tc=512):
    qkv_blocks = 3 * attn_width // tc
    cblocks = conv_width // tc
    return pl.pallas_call(
        _conv_kernel,
        grid=(batch, cblocks),
        in_specs=[pl.BlockSpec((seq, tc), lambda b, j: (b, qkv_blocks + j)),
                  pl.BlockSpec((seq, tc), lambda b, j: (b, qkv_blocks + cblocks + j)),
                  pl.BlockSpec((seq, tc), lambda b, j: (b, qkv_blocks + 2 * cblocks + j)),
                  pl.BlockSpec((CONV_K, tc), lambda b, j: (0, j))],
        out_specs=pl.BlockSpec((seq, tc), lambda b, j: (b, j)),
        out_shape=jax.ShapeDtypeStruct((batch * seq, conv_width), BF16),
        compiler_params=_params("arbitrary", "arbitrary"),
        name="gated_conv",
    )(proj, proj, proj, conv_w)


def _out_proj_kernel(attn_ref, conv_ref, wa_ref, wc_ref, x_ref, g_ref, o_ref):
    acc = jnp.dot(attn_ref[...], wa_ref[...], preferred_element_type=F32)
    acc += jnp.dot(conv_ref[...], wc_ref[...], preferred_element_type=F32)
    o_ref[...] = x_ref[...] + g_ref[...] * acc


def _out_proj(attn, conv, w_out, x2d, mod4, gate_idx, seq, tm=1024, tn=1024):
    m, ka = attn.shape
    kc = conv.shape[1]
    assert ka % kc == 0
    n = w_out.shape[1]
    per_batch = seq // tm
    return pl.pallas_call(
        _out_proj_kernel,
        grid=(m // tm, n // tn),
        in_specs=[pl.BlockSpec((tm, ka), lambda i, j: (i, 0)),
                  pl.BlockSpec((tm, kc), lambda i, j: (i, 0)),
                  pl.BlockSpec((ka, tn), lambda i, j: (0, j)),
                  pl.BlockSpec((kc, tn), lambda i, j: (ka // kc, j)),
                  pl.BlockSpec((tm, tn), lambda i, j: (i, j)),
                  pl.BlockSpec((None, None, 1, tn),
                               lambda i, j: (i // per_batch, gate_idx, 0, j))],
        out_specs=pl.BlockSpec((tm, tn), lambda i, j: (i, j)),
        out_shape=jax.ShapeDtypeStruct((m, n), F32),
        compiler_params=_params("arbitrary", "arbitrary"),
        name="out_proj",
    )(attn, conv, w_out, w_out, x2d, mod4)


def _ffn_kernel(x_hbm, xn_ref, gain_ref, sc_ref, sh_ref, scn_ref, shn_ref, gate_ref,
                fgain_ref, wg_ref, wu_ref, wd_ref, o_hbm, h_even, h_odd, acc_s,
                x_buf, o_buf, x_sem, o_sem, *, tm, chunk, ahead):
    i, f = pl.program_id(0), pl.program_id(1)
    n_chunks = tm // chunk
    n_ahead = tm // ahead

    def x_copy(c):
        rows = pl.ds(i * tm + c * chunk, chunk)
        return pltpu.make_async_copy(x_hbm.at[rows], x_buf.at[c % 2], x_sem.at[c % 2])

    def o_copy(c):
        rows = pl.ds(i * tm + c * chunk, chunk)
        return pltpu.make_async_copy(o_buf.at[c % 2], o_hbm.at[rows], o_sem.at[c % 2])

    @pl.when(jnp.logical_and(i == 0, f == 0))
    def _():
        x_copy(0).start()
        for c in range(n_chunks):
            if c + 1 < n_chunks:
                x_copy(c + 1).start()
            x_copy(c).wait()
            _norm_mod_rows(x_buf.at[c % 2], h_even, c * chunk, chunk,
                           gain_ref[...], sc_ref[...], sh_ref[...])

    def step(h_cur, h_next, first):
        _norm_mod_rows(xn_ref, h_next, jnp.minimum(f, n_ahead - 1) * ahead, ahead,
                       gain_ref[...], scn_ref[...], shn_ref[...])
        h = h_cur[...]
        g = jnp.dot(h, wg_ref[...], preferred_element_type=F32)
        u = jnp.dot(h, wu_ref[...], preferred_element_type=F32)
        a = (_silu(g) * u).astype(BF16)
        down = jnp.dot(a, wd_ref[...], preferred_element_type=F32)
        if first:
            acc_s[...] = down
        else:
            acc_s[...] += down

    for parity, (h_cur, h_next) in enumerate(((h_even, h_odd), (h_odd, h_even))):
        for first in (True, False):
            pl.when(jnp.logical_and(i % 2 == parity, (f == 0) == first))(
                functools.partial(step, h_cur, h_next, first))

    last = pl.num_programs(1) - 1

    @pl.when(f == last - 1)
    def _():
        for c in range(min(2, n_chunks)):
            x_copy(c).start()

    @pl.when(f == last)
    def _():
        for c in range(n_chunks):
            x_copy(c).wait()
            if c >= 2:
                o_copy(c - 2).wait()
            x_c, o_c = x_buf.at[c % 2], o_buf.at[c % 2]

            def residual_norm(rows, c=c, x_c=x_c, o_c=o_c):
                acc = acc_s[pl.ds(c * chunk + rows.start, rows.size), :]
                o_c[rows, :] = _rms(x_c[rows, :] + gate_ref[...] * acc) * fgain_ref[...]

            _for_row_groups(chunk, F32_SUBLANES, residual_norm)
            o_copy(c).start()
            if c + 2 < n_chunks:
                x_copy(c + 2).start()
        for c in range(max(n_chunks - 2, 0), n_chunks):
            o_copy(c).wait()


def _ffn(x1, gain, mod4, sc_idx, sh_idx, gate_idx, final_g, w_gate, w_up, w_down, seq,
         tm=1024, tf=256, chunk=128):
    m, d = x1.shape
    d_ff = w_gate.shape[1]
    n_blocks, n_steps = m // tm, d_ff // tf
    assert n_steps >= 2
    per_batch = seq // tm
    n_ahead = 1
    while n_ahead * 2 <= n_steps and tm % (n_ahead * 2 * BF16_SUBLANES) == 0:
        n_ahead *= 2
    ahead = tm // n_ahead
    row = pl.BlockSpec((1, d), lambda i, f: (0, 0))

    def next_block(i):
        return jnp.minimum(i + 1, n_blocks - 1)

    def mod_spec(idx, block=lambda i: i):
        return pl.BlockSpec((None, None, 1, d),
                            lambda i, f: (block(i) // per_batch, idx, 0, 0))

    return pl.pallas_call(
        functools.partial(_ffn_kernel, tm=tm, chunk=chunk, ahead=ahead),
        grid=(n_blocks, n_steps),
        in_specs=[pl.BlockSpec(memory_space=pl.ANY),
                  pl.BlockSpec((ahead, d), lambda i, f: (
                      next_block(i) * n_ahead + jnp.minimum(f, n_ahead - 1), 0)),
                  row, mod_spec(sc_idx), mod_spec(sh_idx),
                  mod_spec(sc_idx, next_block), mod_spec(sh_idx, next_block),
                  mod_spec(gate_idx), row,
                  pl.BlockSpec((d, tf), lambda i, f: (0, f)),
                  pl.BlockSpec((d, tf), lambda i, f: (0, f)),
                  pl.BlockSpec((tf, d), lambda i, f: (f, 0))],
        out_specs=pl.BlockSpec(memory_space=pl.ANY),
        out_shape=jax.ShapeDtypeStruct((m, d), F32),
        scratch_shapes=[pltpu.VMEM((tm, d), BF16), pltpu.VMEM((tm, d), BF16),
                        pltpu.VMEM((tm, d), F32),
                        pltpu.VMEM((2, chunk, d), F32), pltpu.VMEM((2, chunk, d), F32),
                        pltpu.SemaphoreType.DMA((2,)), pltpu.SemaphoreType.DMA((2,))],
        compiler_params=_params("arbitrary", "arbitrary"),
        name="swiglu",
    )(x1, x1, gain, mod4, mod4, mod4, mod4, mod4, final_g, w_gate, w_up, w_down)


def _rope_tables(seq):
    half = ATTN_HEAD_DIM // 2
    inv_freq = ROPE_THETA ** (-jnp.arange(half, dtype=F32) / half)
    ang = jnp.arange(seq, dtype=F32)[:, None] * inv_freq[None, :]
    cos, sin, zero = jnp.cos(ang), jnp.sin(ang), jnp.zeros_like(ang)
    reps = ATTN_V_DIM // ATTN_HEAD_DIM
    cos_t = jnp.tile(cos, (1, 2 * reps))
    sin_lo = jnp.tile(jnp.concatenate([-sin, zero], axis=1), (1, reps))
    sin_hi = jnp.tile(jnp.concatenate([zero, sin], axis=1), (1, reps))
    return cos_t, sin_lo, sin_hi


def kernel(x, c, w_ada, b_ada, norm1_g, w_in, lambda_q1, lambda_k1, lambda_q2, lambda_k2,
           subln_g, conv_w, w_out, norm2_g, w_gate, w_up, w_down, final_g):
    batch, seq, d = x.shape
    depth = w_ada.shape[0]
    assert depth == 1
    conv_width = conv_w.shape[-1]
    mix_width = w_out.shape[1]
    attn_width = mix_width - conv_width
    n_heads = attn_width // ATTN_V_DIM
    x2d = x.reshape(batch * seq, d)
    tables = _rope_tables(seq)

    for l in range(depth):
        c_pad = jnp.pad(c, ((0, F32_SUBLANES - batch), (0, 0)))
        b_row = b_ada[l][None, :]
        n_early = 2
        mod_a = _ada(c_pad, w_ada[l], b_row, n_early * d)
        mod_a = mod_a[:batch].reshape(batch, n_early, 1, d)

        proj, (w_out_b, w_down_b) = _in_proj(x2d, norm1_g[l][None, :], mod_a, 1, 0, w_in[l],
                                             (w_out[l], w_down[l]), seq)
        lams = [v[l][None, :] for v in (lambda_q1, lambda_k1, lambda_q2, lambda_k2)]
        attn, mod_b, (w_gate_b, w_up_b) = _attention(
            proj, tables, lams, subln_g[l][None, :], (c_pad, w_ada[l], b_row, n_early * d),
            (w_gate[l], w_up[l]), batch, seq, n_heads)
        mod_b = mod_b[:batch].reshape(batch, N_MOD - n_early, 1, d)
        conv = _conv(proj, conv_w[l], batch, seq, attn_width, conv_width)
        x2d = _out_proj(attn, conv, w_out_b, x2d, mod_b, 0, seq)

        x2d = _ffn(x2d, norm2_g[l][None, :], mod_b, 2, 1, 3, final_g[None, :],
                   w_gate_b, w_up_b, w_down_b, seq)

    return x2d.reshape(batch, seq, d)
```

```python
import functools
import math

import jax
import jax.numpy as jnp
from jax import lax
from jax.experimental import pallas as pl
from jax.experimental.pallas import tpu as pltpu

F32 = jnp.float32
BF16 = jnp.bfloat16

ATTN_HEAD_DIM = 64
ATTN_V_DIM = 2 * ATTN_HEAD_DIM
CONV_K = 3
ROPE_THETA = 10000.0
NORM_EPS = 1e-6
N_MOD = 6
LAMBDA_INIT = 0.8 - 0.6 * math.exp(-0.3 * 0)
LOG2_E = math.log2(math.e)

VMEM_LIMIT_BYTES = 60 * 1024 * 1024
LANES = 128
F32_SUBLANES = 8
BF16_SUBLANES = 16


def _params(*semantics):
    return pltpu.CompilerParams(dimension_semantics=semantics,
                                vmem_limit_bytes=VMEM_LIMIT_BYTES)


def _silu(v):
    return v * jax.nn.sigmoid(v)


def _ada_tile(c_ref, w_ref, b_ref):
    c_act = _silu(c_ref[...]).astype(BF16)
    return jnp.dot(c_act, w_ref[...].astype(BF16), preferred_element_type=F32) + b_ref[...]


def _ada_kernel(c_ref, w_ref, b_ref, o_ref):
    o_ref[...] = _ada_tile(c_ref, w_ref, b_ref)


def _ada(c_pad, w_ada, b_ada, n_cols, tn=512):
    rows, d = c_pad.shape
    return pl.pallas_call(
        _ada_kernel,
        grid=(n_cols // tn,),
        in_specs=[pl.BlockSpec((rows, d), lambda j: (0, 0)),
                  pl.BlockSpec((d, tn), lambda j: (0, j)),
                  pl.BlockSpec((1, tn), lambda j: (0, j))],
        out_specs=pl.BlockSpec((rows, tn), lambda j: (0, j)),
        out_shape=jax.ShapeDtypeStruct((rows, n_cols), F32),
        compiler_params=_params("arbitrary"),
        name="ada_mod",
    )(c_pad, w_ada, b_ada)


def _rms(x):
    return x * lax.rsqrt(jnp.mean(x * x, axis=-1, keepdims=True) + NORM_EPS)


def _for_row_groups(n_rows, group, body):
    for start in range(0, n_rows, group):
        body(pl.ds(start, group))


def _norm_mod_rows(x_ref, o_ref, o_row0, n_rows, gain, scale, shift):
    mult = gain * (1.0 + scale)

    def body(rows):
        y = _rms(x_ref[rows, :])
        o_ref[pl.ds(o_row0 + rows.start, rows.size), :] = (y * mult + shift).astype(o_ref.dtype)

    _for_row_groups(n_rows, BF16_SUBLANES, body)


def _split_blocks(shape, n_steps):
    rows, cols = shape
    for rb in range(n_steps, 0, -1):
        cb = n_steps // rb
        if (rb * cb == n_steps and rows % (rb * BF16_SUBLANES) == 0
                and cols % (cb * LANES) == 0):
            return rb, cb
    raise ValueError(f"cannot split {shape} into {n_steps} blocks")


def _cast_specs(weights, n_steps, step_of):
    specs = []
    for w in weights:
        rb, cb = _split_blocks(w.shape, n_steps)
        specs.append(pl.BlockSpec(
            (w.shape[0] // rb, w.shape[1] // cb),
            lambda *g, cb=cb: (step_of(*g) // cb, step_of(*g) % cb)))
    return specs


def _in_proj_kernel(x_ref, gain_ref, sc_ref, sh_ref, w_ref, *rest, chunk, n_cast):
    cast_src, o_ref, cast_dst = rest[:n_cast], rest[n_cast], rest[n_cast + 1:2 * n_cast + 1]
    h_even, h_odd = rest[2 * n_cast + 1:]
    i, j = pl.program_id(0), pl.program_id(1)

    def norm_chunk(h_write):
        _norm_mod_rows(x_ref, h_write, j * chunk, chunk,
                       gain_ref[...], sc_ref[...], sh_ref[...])

    def step(h_write, h_read):
        norm_chunk(h_write)
        for src, dst in zip(cast_src, cast_dst):
            dst[...] = src[...].astype(dst.dtype)
        o_ref[...] = jnp.dot(h_read[...], w_ref[...].astype(BF16),
                             preferred_element_type=F32).astype(o_ref.dtype)

    @pl.when(i == 0)
    def _():
        norm_chunk(h_even)

    @pl.when(jnp.logical_and(i > 0, i % 2 == 0))
    def _():
        step(h_even, h_odd)

    @pl.when(i % 2 == 1)
    def _():
        step(h_odd, h_even)


def _in_proj(x2d, gain, mod4, sc_idx, sh_idx, w, cast_weights, seq, tm=1024, tn=768):
    m, d = x2d.shape
    n = w.shape[1]
    n_blocks, n_tiles = m // tm, n // tn
    chunk = tm // n_tiles
    assert chunk * n_tiles == tm and chunk % BF16_SUBLANES == 0
    per_batch = seq // tm

    def norm_block(i):
        return jnp.minimum(i, n_blocks - 1)

    def tile(i, j):
        return j * jnp.minimum(i, 1)

    def mult_step(i, j):
        return jnp.maximum(i - 1, 0) * n_tiles + tile(i, j)

    def mod_spec(idx):
        return pl.BlockSpec((None, None, 1, d),
                            lambda i, j: (norm_block(i) // per_batch, idx, 0, 0))

    cast_specs = _cast_specs(cast_weights, n_blocks * n_tiles, mult_step)
    outs = pl.pallas_call(
        functools.partial(_in_proj_kernel, chunk=chunk, n_cast=len(cast_weights)),
        grid=(n_blocks + 1, n_tiles),
        in_specs=[pl.BlockSpec((chunk, d), lambda i, j: (norm_block(i) * n_tiles + j, 0)),
                  pl.BlockSpec((1, d), lambda i, j: (0, 0)),
                  mod_spec(sc_idx), mod_spec(sh_idx),
                  pl.BlockSpec((d, tn), lambda i, j: (0, tile(i, j)))] + cast_specs,
        out_specs=[pl.BlockSpec((tm, tn), lambda i, j: (jnp.maximum(i - 1, 0), tile(i, j)))]
        + cast_specs,
        out_shape=[jax.ShapeDtypeStruct((m, n), BF16)]
        + [jax.ShapeDtypeStruct(cw.shape, BF16) for cw in cast_weights],
        scratch_shapes=[pltpu.VMEM((tm, d), BF16), pltpu.VMEM((tm, d), BF16)],
        compiler_params=_params("arbitrary", "arbitrary"),
        name="in_proj",
    )(x2d, gain, mod4, mod4, w, *cast_weights)
    return outs[0], outs[1:]


def _reduce_keys(t, op, group=64):
    keys, queries = t.shape
    partial = op(t.reshape(keys // group, group, queries), axis=0)
    return op(partial, axis=0, keepdims=True)


def _attn_kernel(q_ref, k_ref, v_ref, cos_ref, sa_ref, sb_ref, lq1_ref, lk1_ref,
                 lq2_ref, lk2_ref, g_ref, c_ref, wada_ref, bada_ref, *rest, tq, n_cast):
    cast_src, o_ref, mod_ref = rest[:n_cast], rest[n_cast], rest[n_cast + 1]
    cast_dst = rest[n_cast + 2:2 * n_cast + 2]
    q1_s, q2_s, k_s, vt_s = rest[2 * n_cast + 2:]
    for src, dst in zip(cast_src, cast_dst):
        dst[...] = src[...].astype(dst.dtype)
    mod_ref[...] = _ada_tile(c_ref, wada_ref, bada_ref)
    seq, width = q_ref.shape
    half = ATTN_HEAD_DIM // 2
    cos, sa, sb = cos_ref[...], sa_ref[...], sb_ref[...]

    def rope(t):
        return (t * cos + pltpu.roll(t, width - half, 1) * sa
                + pltpu.roll(t, half, 1) * sb)

    lane = lax.broadcasted_iota(jnp.int32, (seq, width), 1)
    first = lane < ATTN_HEAD_DIM
    q = rope(q_ref[...].astype(F32)) * (ATTN_HEAD_DIM ** -0.5 * LOG2_E)
    q1_s[...] = jnp.where(first, q, 0.0).astype(BF16)
    q2_s[...] = jnp.where(first, 0.0, q).astype(BF16)
    k_s[...] = rope(k_ref[...].astype(F32)).astype(BF16)
    vt_s[0:width, :] = v_ref[...].astype(F32).T.astype(BF16)
    vt_s[width:, :] = jnp.ones((BF16_SUBLANES, seq), BF16)

    lam = (jnp.exp(jnp.sum(lq1_ref[...] * lk1_ref[...], keepdims=True))
           - jnp.exp(jnp.sum(lq2_ref[...] * lk2_ref[...], keepdims=True))
           + LAMBDA_INIT)
    neg = jnp.finfo(F32).min
    gain = g_ref[...] * (1.0 - LAMBDA_INIT)

    hq = tq // 2
    contract_last = (((1,), (1,)), ((), ()))
    for qi in range(seq // tq):
        r0 = qi * tq
        mid, kv_len = r0 + hq, r0 + tq
        qz = jnp.concatenate([q1_s[r0:mid, :], q2_s[r0:mid, :],
                              q1_s[mid:kv_len, :], q2_s[mid:kv_len, :]], axis=0)
        key = lax.broadcasted_iota(jnp.int32, (mid, 2 * tq), 0)
        col = lax.broadcasted_iota(jnp.int32, (mid, 2 * tq), 1)
        query = r0 + jnp.where(col < hq, col, jnp.where(col < tq, col - hq, tq))
        s_top = lax.dot_general(k_s[0:mid, :], qz, contract_last,
                                preferred_element_type=F32)
        s_top = jnp.where(key <= query, s_top, neg)
        key_b = mid + lax.broadcasted_iota(jnp.int32, (hq, tq), 0)
        col_b = lax.broadcasted_iota(jnp.int32, (hq, tq), 1)
        query_b = mid + jnp.where(col_b < hq, col_b, col_b - hq)
        s_bot = lax.dot_general(k_s[mid:kv_len, :], qz[tq:, :], contract_last,
                                preferred_element_type=F32)
        s_bot = jnp.where(key_b <= query_b, s_bot, neg)
        m_top = _reduce_keys(s_top, jnp.max)
        m = jnp.concatenate(
            [m_top[:, :tq], jnp.maximum(m_top[:, tq:], _reduce_keys(s_bot, jnp.max))], axis=1)
        e_top = jnp.exp2(s_top - m).astype(BF16)
        e_bot = jnp.exp2(s_bot - m[:, tq:]).astype(BF16)
        ov = jnp.dot(vt_s[:, 0:mid], e_top, preferred_element_type=F32)
        ov_b = ov[:, tq:] + jnp.dot(vt_s[:, mid:kv_len], e_bot, preferred_element_type=F32)
        ov = jnp.concatenate([ov[:, :tq], ov_b], axis=1)
        ov = ov[:width, :] * (1.0 / ov[width:width + 1, :])
        for half in range(2):
            c0 = half * tq
            o = ov[:, c0:c0 + hq] - lam * ov[:, c0 + hq:c0 + tq]
            o = o * lax.rsqrt(jnp.mean(o * o, axis=0, keepdims=True) + NORM_EPS)
            rows = pl.ds(r0 + half * hq, hq)
            o_ref[rows, :] = (o.T * gain).astype(o_ref.dtype)


def _attention(proj, tables, lams, subln_g, ada, cast_weights, batch, seq, n_heads, tq=512):
    dv = ATTN_V_DIM
    cos, sa, sb = tables
    n_steps = batch * n_heads
    full = lambda b, h: (0, 0)
    vec = pl.BlockSpec((1, ATTN_HEAD_DIM), full)
    cast_specs = _cast_specs(cast_weights, n_steps, lambda b, h: b * n_heads + h)
    n_cast = len(cast_weights)
    c_pad, w_ada, b_ada, col0 = ada
    rows, d = c_pad.shape
    mod_cols = w_ada.shape[1] - col0
    tn = mod_cols // n_steps
    assert tn * n_steps == mod_cols and tn % LANES == 0 and col0 % tn == 0
    ada_tile = lambda b, h: (0, col0 // tn + b * n_heads + h)
    outs = pl.pallas_call(
        functools.partial(_attn_kernel, tq=tq, n_cast=n_cast),
        grid=(batch, n_heads),
        in_specs=[pl.BlockSpec((seq, dv), lambda b, h: (b, h)),
                  pl.BlockSpec((seq, dv), lambda b, h: (b, n_heads + h)),
                  pl.BlockSpec((seq, dv), lambda b, h: (b, 2 * n_heads + h)),
                  pl.BlockSpec((seq, dv), full), pl.BlockSpec((seq, dv), full),
                  pl.BlockSpec((seq, dv), full),
                  vec, vec, vec, vec,
                  pl.BlockSpec((1, dv), full),
                  pl.BlockSpec((rows, d), full), pl.BlockSpec((d, tn), ada_tile),
                  pl.BlockSpec((1, tn), ada_tile)] + cast_specs,
        out_specs=[pl.BlockSpec((seq, dv), lambda b, h: (b, h)),
                   pl.BlockSpec((rows, tn), lambda b, h: (0, b * n_heads + h))] + cast_specs,
        out_shape=[jax.ShapeDtypeStruct((batch * seq, n_heads * dv), BF16),
                   jax.ShapeDtypeStruct((rows, mod_cols), F32)]
        + [jax.ShapeDtypeStruct(w.shape, BF16) for w in cast_weights],
        scratch_shapes=[pltpu.VMEM((seq, dv), BF16)] * 3
        + [pltpu.VMEM((dv + BF16_SUBLANES, seq), BF16)],
        compiler_params=_params("arbitrary", "arbitrary"),
        name="diff_attn",
    )(proj, proj, proj, cos, sa, sb, *lams, subln_g, c_pad, w_ada, b_ada, *cast_weights)
    return outs[0], outs[1], outs[2:]


def _conv_kernel(bg_ref, cg_ref, xg_ref, w_ref, o_ref):
    u = cg_ref[...].astype(F32) * xg_ref[...].astype(F32)
    w = w_ref[...]
    row = lax.broadcasted_iota(jnp.int32, u.shape, 0)
    y = w[CONV_K - 1:CONV_K, :] * u
    for shift in range(1, CONV_K):
        shifted = jnp.where(row >= shift, pltpu.roll(u, shift, 0), 0.0)
        y = y + w[CONV_K - 1 - shift:CONV_K - shift, :] * shifted
    o_ref[...] = (bg_ref[...].astype(F32) * y).astype(o_ref.dtype)


def _conv(proj, conv_w, batch, seq, attn_width, conv_width, tc=512):
    qkv_blocks = 3 * attn_width // tc
    cblocks = conv_width // tc
    return pl.pallas_call(
        _conv_kernel,
        grid=(batch, cblocks),
        in_specs=[pl.BlockSpec((seq, tc), lambda b, j: (b, qkv_blocks + j)),
                  pl.BlockSpec((seq, tc), lambda b, j: (b, qkv_blocks + cblocks + j)),
                  pl.BlockSpec((seq, tc), lambda b, j: (b, qkv_blocks + 2 * cblocks + j)),
                  pl.BlockSpec((CONV_K, tc), lambda b, j: (0, j))],
        out_specs=pl.BlockSpec((seq, tc), lambda b, j: (b, j)),
        out_shape=jax.ShapeDtypeStruct((batch * seq, conv_width), BF16),
        compiler_params=_params("arbitrary", "arbitrary"),
        name="gated_conv",
    )(proj, proj, proj, conv_w)


def _out_proj_kernel(attn_ref, conv_ref, wa_ref, wc_ref, x_ref, g_ref, o_ref):
    acc = jnp.dot(attn_ref[...], wa_ref[...], preferred_element_type=F32)
    acc += jnp.dot(conv_ref[...], wc_ref[...], preferred_element_type=F32)
    o_ref[...] = x_ref[...] + g_ref[...] * acc


def _out_proj(attn, conv, w_out, x2d, mod4, gate_idx, seq, tm=1024, tn=1024):
    m, ka = attn.shape
    kc = conv.shape[1]
    assert ka % kc == 0
    n = w_out.shape[1]
    per_batch = seq // tm
    return pl.pallas_call(
        _out_proj_kernel,
        grid=(m // tm, n // tn),
        in_specs=[pl.BlockSpec((tm, ka), lambda i, j: (i, 0)),
                  pl.BlockSpec((tm, kc), lambda i, j: (i, 0)),
                  pl.BlockSpec((ka, tn), lambda i, j: (0, j)),
                  pl.BlockSpec((kc, tn), lambda i, j: (ka // kc, j)),
                  pl.BlockSpec((tm, tn), lambda i, j: (i, j)),
                  pl.BlockSpec((None, None, 1, tn),
                               lambda i, j: (i // per_batch, gate_idx, 0, j))],
        out_specs=pl.BlockSpec((tm, tn), lambda i, j: (i, j)),
        out_shape=jax.ShapeDtypeStruct((m, n), F32),
        compiler_params=_params("arbitrary", "arbitrary"),
        name="out_proj",
    )(attn, conv, w_out, w_out, x2d, mod4)


def _ffn_kernel(x_hbm, xn_ref, gain_ref, sc_ref, sh_ref, scn_ref, shn_ref, gate_ref,
                fgain_ref, wg_ref, wu_ref, wd_ref, o_hbm, h_even, h_odd, acc_s,
                x_buf, o_buf, x_sem, o_sem, *, tm, chunk, ahead):
    i, f = pl.program_id(0), pl.program_id(1)
    n_chunks = tm // chunk
    n_ahead = tm // ahead

    def x_copy(c):
        rows = pl.ds(i * tm + c * chunk, chunk)
        return pltpu.make_async_copy(x_hbm.at[rows], x_buf.at[c % 2], x_sem.at[c % 2])

    def o_copy(c):
        rows = pl.ds(i * tm + c * chunk, chunk)
        return pltpu.make_async_copy(o_buf.at[c % 2], o_hbm.at[rows], o_sem.at[c % 2])

    @pl.when(jnp.logical_and(i == 0, f == 0))
    def _():
        x_copy(0).start()
        for c in range(n_chunks):
            if c + 1 < n_chunks:
                x_copy(c + 1).start()
            x_copy(c).wait()
            _norm_mod_rows(x_buf.at[c % 2], h_even, c * chunk, chunk,
                           gain_ref[...], sc_ref[...], sh_ref[...])

    def step(h_cur, h_next, first):
        _norm_mod_rows(xn_ref, h_next, jnp.minimum(f, n_ahead - 1) * ahead, ahead,
                       gain_ref[...], scn_ref[...], shn_ref[...])
        h = h_cur[...]
        g = jnp.dot(h, wg_ref[...], preferred_element_type=F32)
        u = jnp.dot(h, wu_ref[...], preferred_element_type=F32)
        a = (_silu(g) * u).astype(BF16)
        down = jnp.dot(a, wd_ref[...], preferred_element_type=F32)
        if first:
            acc_s[...] = down
        else:
            acc_s[...] += down

    for parity, (h_cur, h_next) in enumerate(((h_even, h_odd), (h_odd, h_even))):
        for first in (True, False):
            pl.when(jnp.logical_and(i % 2 == parity, (f == 0) == first))(
                functools.partial(step, h_cur, h_next, first))

    last = pl.num_programs(1) - 1

    @pl.when(f == last - 1)
    def _():
        for c in range(min(2, n_chunks)):
            x_copy(c).start()

    @pl.when(f == last)
    def _():
        for c in range(n_chunks):
            x_copy(c).wait()
            if c >= 2:
                o_copy(c - 2).wait()
            x_c, o_c = x_buf.at[c % 2], o_buf.at[c % 2]

            def residual_norm(rows, c=c, x_c=x_c, o_c=o_c):
                acc = acc_s[pl.ds(c * chunk + rows.start, rows.size), :]
                o_c[rows, :] = _rms(x_c[rows, :] + gate_ref[...] * acc) * fgain_ref[...]

            _for_row_groups(chunk, F32_SUBLANES, residual_norm)
            o_copy(c).start()
            if c + 2 < n_chunks:
                x_copy(c + 2).start()
        for c in range(max(n_chunks - 2, 0), n_chunks):
            o_copy(c).wait()


def _ffn(x1, gain, mod4, sc_idx, sh_idx, gate_idx, final_g, w_gate, w_up, w_down, seq,
         tm=1024, tf=256, chunk=128):
    m, d = x1.shape
    d_ff = w_gate.shape[1]
    n_blocks, n_steps = m // tm, d_ff // tf
    assert n_steps >= 2
    per_batch = seq // tm
    n_ahead = 1
    while n_ahead * 2 <= n_steps and tm % (n_ahead * 2 * BF16_SUBLANES) == 0:
        n_ahead *= 2
    ahead = tm // n_ahead
    row = pl.BlockSpec((1, d), lambda i, f: (0, 0))

    def next_block(i):
        return jnp.minimum(i + 1, n_blocks - 1)

    def mod_spec(idx, block=lambda i: i):
        return pl.BlockSpec((None, None, 1, d),
                            lambda i, f: (block(i) // per_batch, idx, 0, 0))

    return pl.pallas_call(
        functools.partial(_ffn_kernel, tm=tm, chunk=chunk, ahead=ahead),
        grid=(n_blocks, n_steps),
        in_specs=[pl.BlockSpec(memory_space=pl.ANY),
                  pl.BlockSpec((ahead, d), lambda i, f: (
                      next_block(i) * n_ahead + jnp.minimum(f, n_ahead - 1), 0)),
                  row, mod_spec(sc_idx), mod_spec(sh_idx),
                  mod_spec(sc_idx, next_block), mod_spec(sh_idx, next_block),
                  mod_spec(gate_idx), row,
                  pl.BlockSpec((d, tf), lambda i, f: (0, f)),
                  pl.BlockSpec((d, tf), lambda i, f: (0, f)),
                  pl.BlockSpec((tf, d), lambda i, f: (f, 0))],
        out_specs=pl.BlockSpec(memory_space=pl.ANY),
        out_shape=jax.ShapeDtypeStruct((m, d), F32),
        scratch_shapes=[pltpu.VMEM((tm, d), BF16), pltpu.VMEM((tm, d), BF16),
                        pltpu.VMEM((tm, d), F32),
                        pltpu.VMEM((2, chunk, d), F32), pltpu.VMEM((2, chunk, d), F32),
                        pltpu.SemaphoreType.DMA((2,)), pltpu.SemaphoreType.DMA((2,))],
        compiler_params=_params("arbitrary", "arbitrary"),
        name="swiglu",
    )(x1, x1, gain, mod4, mod4, mod4, mod4, mod4, final_g, w_gate, w_up, w_down)


def _rope_tables(seq):
    half = ATTN_HEAD_DIM // 2
    inv_freq = ROPE_THETA ** (-jnp.arange(half, dtype=F32) / half)
    ang = jnp.arange(seq, dtype=F32)[:, None] * inv_freq[None, :]
    cos, sin, zero = jnp.cos(ang), jnp.sin(ang), jnp.zeros_like(ang)
    reps = ATTN_V_DIM // ATTN_HEAD_DIM
    cos_t = jnp.tile(cos, (1, 2 * reps))
    sin_lo = jnp.tile(jnp.concatenate([-sin, zero], axis=1), (1, reps))
    sin_hi = jnp.tile(jnp.concatenate([zero, sin], axis=1), (1, reps))
    return cos_t, sin_lo, sin_hi


def kernel(x, c, w_ada, b_ada, norm1_g, w_in, lambda_q1, lambda_k1, lambda_q2, lambda_k2,
           subln_g, conv_w, w_out, norm2_g, w_gate, w_up, w_down, final_g):
    batch, seq, d = x.shape
    depth = w_ada.shape[0]
    assert depth == 1
    conv_width = conv_w.shape[-1]
    mix_width = w_out.shape[1]
    attn_width = mix_width - conv_width
    n_heads = attn_width // ATTN_V_DIM
    x2d = x.reshape(batch * seq, d)
    tables = _rope_tables(seq)

    for l in range(depth):
        c_pad = jnp.pad(c, ((0, F32_SUBLANES - batch), (0, 0)))
        b_row = b_ada[l][None, :]
        n_early = 2
        mod_a = _ada(c_pad, w_ada[l], b_row, n_early * d)
        mod_a = mod_a[:batch].reshape(batch, n_early, 1, d)

        proj, (w_out_b, w_down_b) = _in_proj(x2d, norm1_g[l][None, :], mod_a, 1, 0, w_in[l],
                                             (w_out[l], w_down[l]), seq)
        lams = [v[l][None, :] for v in (lambda_q1, lambda_k1, lambda_q2, lambda_k2)]
        attn, mod_b, (w_gate_b, w_up_b) = _attention(
            proj, tables, lams, subln_g[l][None, :], (c_pad, w_ada[l], b_row, n_early * d),
            (w_gate[l], w_up[l]), batch, seq, n_heads)
        mod_b = mod_b[:batch].reshape(batch, N_MOD - n_early, 1, d)
        conv = _conv(proj, conv_w[l], batch, seq, attn_width, conv_width)
        x2d = _out_proj(attn, conv, w_out_b, x2d, mod_b, 0, seq)

        x2d = _ffn(x2d, norm2_g[l][None, :], mod_b, 2, 1, 3, final_g[None, :],
                   w_gate_b, w_up_b, w_down_b, seq)

    return x2d.reshape(batch, seq, d)
```
